```python
import math
import jax, jax.numpy as jnp
from jax import lax
import numpy as np

D_MODEL = 2048
BATCH = 1
SEQ = 8192
DEPTH = 1

GRID_W = 64
CTX_LEN = 256
N_DIFF_HEADS = D_MODEL // 256
HEAD_DIM = 64
V_HEAD_DIM = 2 * HEAD_DIM
ATTN_WIDTH = N_DIFF_HEADS * V_HEAD_DIM
SGU_WIDTH = D_MODEL // 2
N_SGU_GROUPS = 8
SGU_GROUP_DIM = SGU_WIDTH // N_SGU_GROUPS
CHUNK = 128
Q_BLOCK = 128
FF_DIM = 4 * D_MODEL
IN_WIDTH = 3 * ATTN_WIDTH + 2 * SGU_WIDTH
ROPE_BASE = 10000.0
EPS = 1e-6

kernel_name = "hybrid_diffattn_sgu_dit_block"


def rms_norm(x, g):
    xf = x.astype(jnp.float32)
    y = xf * lax.rsqrt(jnp.mean(xf * xf, axis=-1, keepdims=True) + EPS)
    return (y * g.astype(jnp.float32)).astype(x.dtype)


def layer_norm(x, g, b):
    xf = x.astype(jnp.float32)
    mu = jnp.mean(xf, axis=-1, keepdims=True)
    var = jnp.mean(jnp.square(xf - mu), axis=-1, keepdims=True)
    y = (xf - mu) * lax.rsqrt(var + EPS)
    return (y * g.astype(jnp.float32) + b.astype(jnp.float32)).astype(x.dtype)


def modulated_norm(x, g, shift, scale):
    return rms_norm(x, g) * (1 + scale) + shift


def axial_rope_tables(n_tokens):
    rows = n_tokens // GRID_W
    r, col = jnp.meshgrid(jnp.arange(rows, dtype=jnp.float32),
                          jnp.arange(GRID_W, dtype=jnp.float32), indexing="ij")
    n_freq = HEAD_DIM // 4
    inv = ROPE_BASE ** (-jnp.arange(n_freq, dtype=jnp.float32) / n_freq)
    ang = jnp.concatenate([r.reshape(-1, 1) * inv, col.reshape(-1, 1) * inv], axis=-1)
    return jnp.cos(ang), jnp.sin(ang)


def apply_rope(t, cos, sin):
    tf = t.astype(jnp.float32).reshape(t.shape[:-1] + (HEAD_DIM // 2, 2))
    t1, t2 = tf[..., 0], tf[..., 1]
    cs, sn = cos[None, :, None, :], sin[None, :, None, :]
    out = jnp.stack([t1 * cs - t2 * sn, t1 * sn + t2 * cs], axis=-1)
    return out.reshape(t.shape).astype(t.dtype)


def diff_attention(q, k, v, lam, lam_init, subln_g):
    B, Nq = q.shape[0], q.shape[1]
    Nk = k.shape[1]
    nb = Nq // Q_BLOCK
    scale = HEAD_DIM ** -0.5
    qb = q.reshape(B, nb, Q_BLOCK, 2 * N_DIFF_HEADS, HEAD_DIM).transpose(1, 0, 2, 3, 4)

    def one_block(qblk):
        s = jnp.einsum("bqhd,bkhd->bhqk", qblk, k).astype(jnp.float32) * scale
        p = jax.nn.softmax(s, axis=-1).reshape(B, N_DIFF_HEADS, 2, Q_BLOCK, Nk)
        a = p[:, :, 0] - lam * p[:, :, 1]
        return jnp.einsum("bhqk,bkhe->bqhe", a.astype(v.dtype), v)

    o = lax.map(one_block, qb)
    o = o.transpose(1, 0, 2, 3, 4).reshape(B, Nq, N_DIFF_HEADS, V_HEAD_DIM)
    o = rms_norm(o, subln_g) * (1 - lam_init)
    return o.reshape(B, Nq, ATTN_WIDTH)


def chunk_sgu(u, v, ln_g, ln_b, w_s, b_s):
    B, N, _ = v.shape
    vn = layer_norm(v, ln_g, ln_b).reshape(B, N // CHUNK, CHUNK, N_SGU_GROUPS, SGU_GROUP_DIM)
    mixed = jnp.einsum("gij,bnjgc->bnigc", w_s, vn) + b_s.T[None, None, :, :, None]
    return u * mixed.reshape(B, N, SGU_WIDTH)


def gated_merge(h, attn_o, sgu_o, w_gate, b_gate, w_br_attn, w_br_sgu, w_out):
    gates = jax.nn.sigmoid((h @ w_gate + b_gate).astype(jnp.float32)).astype(h.dtype)
    g_attn, g_sgu = jnp.split(gates, 2, axis=-1)
    return (g_attn * (attn_o @ w_br_attn) + g_sgu * (sgu_o @ w_br_sgu)) @ w_out


def sq_relu_ffn(h, w1, w2):
    return jnp.square(jax.nn.relu(h @ w1)) @ w2


def setup_inputs(seed: int = 0) -> dict:
    key = jax.random.key(seed)
    ks = jax.random.split(key, 26)
    L = DEPTH

    def nrm(k, shape, s):
        return jax.random.normal(k, shape, jnp.float32) * s

    return {
        "x": nrm(ks[0], (BATCH, SEQ, D_MODEL), 1.0),
        "c": nrm(ks[1], (BATCH, D_MODEL), 1.0),
        "ctx": nrm(ks[2], (BATCH, CTX_LEN, D_MODEL), 1.0),
        "c_ctx": nrm(ks[3], (D_MODEL,), 1.0),
        "w_ada": nrm(ks[4], (L, D_MODEL, 6 * D_MODEL), 0.5 * D_MODEL ** -0.5),
        "b_ada": nrm(ks[5], (L, 6 * D_MODEL), 0.01),
        "norm1_g": 1.0 + nrm(ks[6], (L, D_MODEL), 0.02),
        "norm2_g": 1.0 + nrm(ks[7], (L, D_MODEL), 0.02),
        "w_in": nrm(ks[8], (L, D_MODEL, IN_WIDTH), D_MODEL ** -0.5),
        "lam_q1": nrm(ks[9], (L, HEAD_DIM), 0.1),
        "lam_k1": nrm(ks[10], (L, HEAD_DIM), 0.1),
        "lam_q2": nrm(ks[11], (L, HEAD_DIM), 0.1),
        "lam_k2": nrm(ks[12], (L, HEAD_DIM), 0.1),
        "subln_g": 1.0 + nrm(ks[13], (L, V_HEAD_DIM), 0.02),
        "sgu_ln_g": 1.0 + nrm(ks[14], (L, SGU_WIDTH), 0.02),
        "sgu_ln_b": nrm(ks[15], (L, SGU_WIDTH), 0.02),
        "w_spatial": nrm(ks[16], (L, N_SGU_GROUPS, CHUNK, CHUNK), 0.5 * CHUNK ** -0.5),
        "b_spatial": 1.0 + nrm(ks[17], (L, N_SGU_GROUPS, CHUNK), 0.02),
        "w_gate": nrm(ks[18], (L, D_MODEL, 2 * D_MODEL), D_MODEL ** -0.5),
        "b_gate": nrm(ks[19], (L, 2 * D_MODEL), 0.01),
        "w_br_attn": nrm(ks[20], (L, ATTN_WIDTH, D_MODEL), ATTN_WIDTH ** -0.5),
        "w_br_sgu": nrm(ks[21], (L, SGU_WIDTH, D_MODEL), SGU_WIDTH ** -0.5),
        "w_out": nrm(ks[22], (L, D_MODEL, D_MODEL), D_MODEL ** -0.5),
        "w_ff1": nrm(ks[23], (L, D_MODEL, FF_DIM), D_MODEL ** -0.5),
        "w_ff2": nrm(ks[24], (L, FF_DIM, D_MODEL), FF_DIM ** -0.5),
        "final_g": 1.0 + nrm(ks[25], (D_MODEL,), 0.02),
    }


def reference(x, c, ctx, c_ctx, w_ada, b_ada, norm1_g, norm2_g, w_in,
              lam_q1, lam_k1, lam_q2, lam_k2, subln_g, sgu_ln_g, sgu_ln_b,
              w_spatial, b_spatial, w_gate, b_gate, w_br_attn, w_br_sgu, w_out,
              w_ff1, w_ff2, final_g):
    B, N, _ = x.shape
    Lc = ctx.shape[1]
    A = ATTN_WIDTH
    H2 = 2 * N_DIFF_HEADS
    cos, sin = axial_rope_tables(N)
    cx = ctx
    for l in range(DEPTH):
        lam_init = 0.8 - 0.6 * math.exp(-0.3 * l)
        lam = (jnp.exp(jnp.sum((lam_q1[l] * lam_k1[l]).astype(jnp.float32)))
               - jnp.exp(jnp.sum((lam_q2[l] * lam_k2[l]).astype(jnp.float32))) + lam_init)
        mod_x = jax.nn.silu(c) @ w_ada[l] + b_ada[l]
        mod_c = jax.nn.silu(c_ctx) @ w_ada[l] + b_ada[l]
        sh1, sc1, g1, sh2, sc2, g2 = [m[:, None, :] for m in jnp.split(mod_x, 6, axis=-1)]
        csh1, csc1, cg1, csh2, csc2, cg2 = jnp.split(mod_c, 6, axis=-1)

        hx = modulated_norm(x, norm1_g[l], sh1, sc1)
        hc = modulated_norm(cx, norm1_g[l], csh1, csc1)
        w_l = w_in[l]
        proj_x = hx @ w_l
        qx = apply_rope(proj_x[..., :A].reshape(B, N, H2, HEAD_DIM), cos, sin)
        kx = apply_rope(proj_x[..., A:2 * A].reshape(B, N, H2, HEAD_DIM), cos, sin)
        vx = proj_x[..., 2 * A:3 * A].reshape(B, N, N_DIFF_HEADS, V_HEAD_DIM)
        ux, vsx = jnp.split(jax.nn.gelu(proj_x[..., 3 * A:]), 2, axis=-1)
        kc = (hc @ w_l[:, A:2 * A]).reshape(B, Lc, H2, HEAD_DIM)
        vc = (hc @ w_l[:, 2 * A:3 * A]).reshape(B, Lc, N_DIFF_HEADS, V_HEAD_DIM)
        k_all = jnp.concatenate([kc, kx], axis=1)
        v_all = jnp.concatenate([vc, vx], axis=1)
        ax = diff_attention(qx, k_all, v_all, lam, lam_init, subln_g[l])
        sx = chunk_sgu(ux, vsx, sgu_ln_g[l], sgu_ln_b[l], w_spatial[l], b_spatial[l])
        x = x + g1 * gated_merge(hx, ax, sx, w_gate[l], b_gate[l], w_br_attn[l], w_br_sgu[l], w_out[l])

        x = x + g2 * sq_relu_ffn(modulated_norm(x, norm2_g[l], sh2, sc2), w_ff1[l], w_ff2[l])

        if l < DEPTH - 1:
            qc = (hc @ w_l[:, :A]).reshape(B, Lc, H2, HEAD_DIM)
            uc, vsc = jnp.split(jax.nn.gelu(hc @ w_l[:, 3 * A:]), 2, axis=-1)
            ac = diff_attention(qc, kc, vc, lam, lam_init, subln_g[l])
            scx = chunk_sgu(uc, vsc, sgu_ln_g[l], sgu_ln_b[l], w_spatial[l], b_spatial[l])
            cx = cx + cg1 * gated_merge(hc, ac, scx, w_gate[l], b_gate[l], w_br_attn[l], w_br_sgu[l], w_out[l])
            cx = cx + cg2 * sq_relu_ffn(modulated_norm(cx, norm2_g[l], csh2, csc2), w_ff1[l], w_ff2[l])
    return rms_norm(x, final_g)
```

```python
import functools
import math

import jax
import jax.numpy as jnp
from jax import lax
from jax.experimental import pallas as pl
from jax.experimental.pallas import tpu as pltpu

D = 2048
N_TOK = 8192
GRID_W = 64
CTX = 256
N_HEADS = 8
HEAD_DIM = 64
PAIR = 2 * HEAD_DIM
A_W = N_HEADS * PAIR
SGU_W = D // 2
N_GROUPS = 8
GROUP_DIM = SGU_W // N_GROUPS
CHUNK = 128
FF = 4 * D
IN_W = 3 * A_W + 2 * SGU_W
ROPE_BASE = 10000.0
EPS = 1e-6
LAM_INIT = 0.8 - 0.6 * math.exp(-0.3 * 0)

V7X_VMEM_BYTES = 64 * 1024 * 1024
VMEM_LIMIT = V7X_VMEM_BYTES - 8 * 1024 * 1024
LANES = 128

TM_IN = 512
TQ = 256
TK = 512
TM_MERGE = 1024
TN_MERGE = 512
TM_OUT = 512
TM_FF = 512
TF = 1024
TN_ADA = 1024

NEG_BIG = -1e30


def _params(*sem):
    return pltpu.CompilerParams(dimension_semantics=sem, vmem_limit_bytes=VMEM_LIMIT)


def _gelu_tanh(x):
    return 0.5 * x * (1.0 + jnp.tanh(math.sqrt(2.0 / math.pi) * (x + 0.044715 * (x * x * x))))


def _mod_norm(xf, g, shift, scale):
    y = xf * lax.rsqrt(jnp.mean(xf * xf, axis=-1, keepdims=True) + EPS)
    return (y * g) * (1.0 + scale) + shift


def _ada_kernel(ct_ref, w_ref, b_ref, o_ref):
    w = w_ref[...]
    rows = []
    for r in range(2):
        cv = ct_ref[:, r:r + 1]
        a = cv * jax.nn.sigmoid(cv)
        rows.append(jnp.sum(a * w, axis=0, keepdims=True))
    o_ref[...] = jnp.concatenate(rows, axis=0) + b_ref[...]


def _ada(ct, w, b):
    n_out = w.shape[1]
    return pl.pallas_call(
        _ada_kernel,
        grid=(n_out // TN_ADA,),
        in_specs=[
            pl.BlockSpec((D, 2), lambda j: (0, 0)),
            pl.BlockSpec((D, TN_ADA), lambda j: (0, j)),
            pl.BlockSpec((1, TN_ADA), lambda j: (0, j)),
        ],
        out_specs=pl.BlockSpec((2, TN_ADA), lambda j: (0, j)),
        out_shape=jax.ShapeDtypeStruct((2, n_out), jnp.float32),
        compiler_params=_params("parallel"),
    )(ct, w, b)


def _ctx_kernel(x_ref, mod_ref, g_ref, w_ref, k_ref, vt_ref, h_s):
    j = pl.program_id(0)

    @pl.when(j == 0)
    def _():
        h = _mod_norm(x_ref[...], g_ref[...], mod_ref[1:2, 0:D], mod_ref[1:2, D:2 * D])
        h_s[...] = h.astype(jnp.bfloat16)

    acc = jnp.dot(h_s[...], w_ref[...], preferred_element_type=jnp.float32)

    @pl.when(j == 0)
    def _():
        k_ref[...] = acc.astype(jnp.bfloat16)

    @pl.when(j == 1)
    def _():
        vt_ref[...] = acc.T.astype(jnp.bfloat16)


def _ctx_proj(ctx2d, mod, g1, w_in):
    return pl.pallas_call(
        _ctx_kernel,
        grid=(2,),
        in_specs=[
            pl.BlockSpec((CTX, D), lambda j: (0, 0)),
            pl.BlockSpec((2, 6 * D), lambda j: (0, 0)),
            pl.BlockSpec((1, D), lambda j: (0, 0)),
            pl.BlockSpec((D, A_W), lambda j: (0, j + 1)),
        ],
        out_specs=[
            pl.BlockSpec((CTX, A_W), lambda j: (0, 0)),
            pl.BlockSpec((A_W, CTX), lambda j: (0, 0)),
        ],
        out_shape=[
            jax.ShapeDtypeStruct((CTX, A_W), jnp.bfloat16),
            jax.ShapeDtypeStruct((A_W, CTX), jnp.bfloat16),
        ],
        scratch_shapes=[pltpu.VMEM((CTX, D), jnp.bfloat16)],
        compiler_params=_params("arbitrary"),
    )(ctx2d, mod, g1, w_in)


def _rope(t, c, s):
    outs = []
    for h in range(N_HEADS):
        blk = t[:, h * PAIR:(h + 1) * PAIR]
        outs.append(blk * c + pltpu.roll(blk, PAIR // 2, axis=1) * s)
    return jnp.concatenate(outs, axis=1)


def _in_kernel(x_ref, mod_ref, g_ref, w_ref, cos_ref, sin_ref, ws_ref, bst_ref, lng_ref, lnb_ref,
               hx_ref, q_ref, k_ref, vt_ref, sx_ref, u_s):
    j = pl.program_id(1)

    @pl.when(j == 0)
    def _():
        h = _mod_norm(x_ref[...], g_ref[...], mod_ref[0:1, 0:D], mod_ref[0:1, D:2 * D])
        hx_ref[...] = h.astype(jnp.bfloat16)

    acc = jnp.dot(hx_ref[...], w_ref[...], preferred_element_type=jnp.float32)

    @pl.when(j == 0)
    def _():
        q_ref[...] = (_rope(acc, cos_ref[...], sin_ref[...]) * (HEAD_DIM ** -0.5)).astype(jnp.bfloat16)

    @pl.when(j == 1)
    def _():
        k_ref[...] = _rope(acc, cos_ref[...], sin_ref[...]).astype(jnp.bfloat16)

    @pl.when(j == 2)
    def _():
        vt_ref[...] = acc.T.astype(jnp.bfloat16)

    @pl.when(j == 3)
    def _():
        u_s[...] = _gelu_tanh(acc)

    @pl.when(j == 4)
    def _():
        v = _gelu_tanh(acc)
        mu = jnp.mean(v, axis=-1, keepdims=True)
        vc = v - mu
        var = jnp.mean(vc * vc, axis=-1, keepdims=True)
        vn = ((vc * lax.rsqrt(var + EPS)) * lng_ref[...] + lnb_ref[...]).astype(jnp.bfloat16)
        n_chunks = TM_IN // CHUNK
        for g in range(N_GROUPS):
            cols = slice(g * GROUP_DIM, (g + 1) * GROUP_DIM)
            rhs = jnp.concatenate([vn[c * CHUNK:(c + 1) * CHUNK, cols] for c in range(n_chunks)], axis=1)
            mixed = jnp.dot(ws_ref[g], rhs, preferred_element_type=jnp.float32) + bst_ref[:, g:g + 1]
            for c in range(n_chunks):
                rows = slice(c * CHUNK, (c + 1) * CHUNK)
                sx_ref[rows, cols] = (u_s[rows, cols] * mixed[:, c * GROUP_DIM:(c + 1) * GROUP_DIM]
                                      ).astype(jnp.bfloat16)


def _in_proj(x2d, mod, g1, w_in, cos_t, sin_t, ws, bst, lng, lnb):
    n = x2d.shape[0]
    row = lambda i, j: (i, 0)
    const = lambda i, j: (0, 0)
    return pl.pallas_call(
        _in_kernel,
        grid=(n // TM_IN, IN_W // A_W),
        in_specs=[
            pl.BlockSpec((TM_IN, D), row),
            pl.BlockSpec((2, 6 * D), const),
            pl.BlockSpec((1, D), const),
            pl.BlockSpec((D, A_W), lambda i, j: (0, j)),
            pl.BlockSpec((TM_IN, PAIR), row),
            pl.BlockSpec((TM_IN, PAIR), row),
            pl.BlockSpec((N_GROUPS, CHUNK, CHUNK), lambda i, j: (0, 0, 0)),
            pl.BlockSpec((CHUNK, N_GROUPS), const),
            pl.BlockSpec((1, SGU_W), const),
            pl.BlockSpec((1, SGU_W), const),
        ],
        out_specs=[
            pl.BlockSpec((TM_IN, D), row),
            pl.BlockSpec((TM_IN, A_W), row),
            pl.BlockSpec((TM_IN, A_W), row),
            pl.BlockSpec((A_W, TM_IN), lambda i, j: (0, i)),
            pl.BlockSpec((TM_IN, SGU_W), row),
        ],
        out_shape=[
            jax.ShapeDtypeStruct((n, D), jnp.bfloat16),
            jax.ShapeDtypeStruct((n, A_W), jnp.bfloat16),
            jax.ShapeDtypeStruct((n, A_W), jnp.bfloat16),
            jax.ShapeDtypeStruct((A_W, n), jnp.bfloat16),
            jax.ShapeDtypeStruct((n, SGU_W), jnp.bfloat16),
        ],
        scratch_shapes=[pltpu.VMEM((TM_IN, SGU_W), jnp.float32)],
        compiler_params=_params("parallel", "arbitrary"),
    )(x2d, mod, g1, w_in, cos_t, sin_t, ws, bst, lng, lnb)


def _attn_kernel(q_ref, k_ref, vt_ref, kc_ref, vct_ref, lq1_ref, lk1_ref, lq2_ref, lk2_ref, sg_ref,
                 o_ref, m_s, l_s, acc_s):
    q = q_ref[...]
    lane = lax.broadcasted_iota(jnp.int32, q.shape, 1)
    first = (lane % HEAD_DIM) < (HEAD_DIM // 2)
    zero = jnp.zeros_like(q)
    qq = jnp.concatenate([jnp.where(first, q, zero), jnp.where(first, zero, q)], axis=0)

    m_s[...] = jnp.full(m_s.shape, NEG_BIG, jnp.float32)
    l_s[...] = jnp.zeros(l_s.shape, jnp.float32)
    acc_s[...] = jnp.zeros(acc_s.shape, jnp.float32)

    def block(k_blk, vt_blk):
        s = lax.dot_general(k_blk, qq, (((1,), (1,)), ((), ())),
                            preferred_element_type=jnp.float32)
        m_old = m_s[...]
        m_new = jnp.maximum(m_old, jnp.max(s, axis=0, keepdims=True))
        alpha = jnp.exp(m_old - m_new)
        p = jnp.exp(s - m_new)
        l_s[...] = alpha * l_s[...] + jnp.sum(p, axis=0, keepdims=True)
        acc_s[...] = alpha * acc_s[...] + jnp.dot(vt_blk, p.astype(jnp.bfloat16),
                                                  preferred_element_type=jnp.float32)
        m_s[...] = m_new

    block(kc_ref[...], vct_ref[...])

    def body(t, carry):
        start = pl.multiple_of(t * TK, TK)
        block(k_ref[pl.ds(start, TK), :], vt_ref[:, pl.ds(start, TK)])
        return carry

    lax.fori_loop(0, k_ref.shape[0] // TK, body, 0)

    lam = (jnp.exp(jnp.sum(lq1_ref[...] * lk1_ref[...], axis=-1, keepdims=True))
           - jnp.exp(jnp.sum(lq2_ref[...] * lk2_ref[...], axis=-1, keepdims=True)) + LAM_INIT)
    o_both = acc_s[...] * (1.0 / l_s[...])
    o = o_both[:, :TQ] - lam * o_both[:, TQ:]
    y = o * lax.rsqrt(jnp.mean(o * o, axis=0, keepdims=True) + EPS)
    y = (y * sg_ref[...]) * (1.0 - LAM_INIT)
    o_ref[...] = y.T.astype(jnp.bfloat16)


def _attn(q, k, vt, kc, vct, lq1, lk1, lq2, lk2, sg_col):
    n = q.shape[0]
    vec = pl.BlockSpec((1, HEAD_DIM), lambda h, i: (0, 0))
    return pl.pallas_call(
        _attn_kernel,
        grid=(N_HEADS, n // TQ),
        in_specs=[
            pl.BlockSpec((TQ, PAIR), lambda h, i: (i, h)),
            pl.BlockSpec((n, PAIR), lambda h, i: (0, h)),
            pl.BlockSpec((PAIR, n), lambda h, i: (h, 0)),
            pl.BlockSpec((CTX, PAIR), lambda h, i: (0, h)),
            pl.BlockSpec((PAIR, CTX), lambda h, i: (h, 0)),
            vec, vec, vec, vec,
            pl.BlockSpec((PAIR, 1), lambda h, i: (0, 0)),
        ],
        out_specs=pl.BlockSpec((TQ, PAIR), lambda h, i: (i, h)),
        out_shape=jax.ShapeDtypeStruct((n, A_W), jnp.bfloat16),
        scratch_shapes=[
            pltpu.VMEM((1, 2 * TQ), jnp.float32),
            pltpu.VMEM((1, 2 * TQ), jnp.float32),
            pltpu.VMEM((PAIR, 2 * TQ), jnp.float32),
        ],
        compiler_params=_params("parallel", "parallel"),
    )(q, k, vt, kc, vct, lq1, lk1, lq2, lk2, sg_col)


def _merge_kernel(hx_ref, ax_ref, sx_ref, wga_ref, wgs_ref, bga_ref, bgs_ref, wba_ref, wbs_ref, m_ref):
    hx = hx_ref[...]
    ga = jax.nn.sigmoid(jnp.dot(hx, wga_ref[...], preferred_element_type=jnp.float32) + bga_ref[...])
    gs = jax.nn.sigmoid(jnp.dot(hx, wgs_ref[...], preferred_element_type=jnp.float32) + bgs_ref[...])
    ba = jnp.dot(ax_ref[...], wba_ref[...], preferred_element_type=jnp.float32)
    bs = jnp.dot(sx_ref[...], wbs_ref[...], preferred_element_type=jnp.float32)
    m_ref[...] = (ga * ba + gs * bs).astype(jnp.bfloat16)


def _merge(hx, ax, sx, w_gate, b_gate, w_ba, w_bs):
    n = hx.shape[0]
    nj = D // TN_MERGE
    row = lambda i, j: (i, 0)
    return pl.pallas_call(
        _merge_kernel,
        grid=(n // TM_MERGE, nj),
        in_specs=[
            pl.BlockSpec((TM_MERGE, D), row),
            pl.BlockSpec((TM_MERGE, A_W), row),
            pl.BlockSpec((TM_MERGE, SGU_W), row),
            pl.BlockSpec((D, TN_MERGE), lambda i, j: (0, j)),
            pl.BlockSpec((D, TN_MERGE), lambda i, j: (0, j + nj)),
            pl.BlockSpec((1, TN_MERGE), lambda i, j: (0, j)),
            pl.BlockSpec((1, TN_MERGE), lambda i, j: (0, j + nj)),
            pl.BlockSpec((A_W, TN_MERGE), lambda i, j: (0, j)),
            pl.BlockSpec((SGU_W, TN_MERGE), lambda i, j: (0, j)),
        ],
        out_specs=pl.BlockSpec((TM_MERGE, TN_MERGE), lambda i, j: (i, j)),
        out_shape=jax.ShapeDtypeStruct((n, D), jnp.bfloat16),
        compiler_params=_params("parallel", "arbitrary"),
    )(hx, ax, sx, w_gate, w_gate, b_gate, b_gate, w_ba, w_bs)


def _out_kernel(m_ref, w_ref, x_ref, mod_ref, g_ref, x1_ref, h2_ref):
    y = jnp.dot(m_ref[...], w_ref[...], preferred_element_type=jnp.float32)
    x1 = x_ref[...] + mod_ref[0:1, 2 * D:3 * D] * y
    x1_ref[...] = x1
    h2 = _mod_norm(x1, g_ref[...], mod_ref[0:1, 3 * D:4 * D], mod_ref[0:1, 4 * D:5 * D])
    h2_ref[...] = h2.astype(jnp.bfloat16)


def _out_proj(m, w_out, x2d, mod, g2):
    n = m.shape[0]
    row = lambda i: (i, 0)
    const = lambda i: (0, 0)
    return pl.pallas_call(
        _out_kernel,
        grid=(n // TM_OUT,),
        in_specs=[
            pl.BlockSpec((TM_OUT, D), row),
            pl.BlockSpec((D, D), const),
            pl.BlockSpec((TM_OUT, D), row),
            pl.BlockSpec((2, 6 * D), const),
            pl.BlockSpec((1, D), const),
        ],
        out_specs=[pl.BlockSpec((TM_OUT, D), row), pl.BlockSpec((TM_OUT, D), row)],
        out_shape=[jax.ShapeDtypeStruct((n, D), jnp.float32), jax.ShapeDtypeStruct((n, D), jnp.bfloat16)],
        compiler_params=_params("parallel"),
    )(m, w_out, x2d, mod, g2)


def _ffn_kernel(h_ref, w1_ref, w2_ref, x1_ref, mod_ref, fg_ref, o_ref, acc_s):
    f = pl.program_id(1)

    @pl.when(f == 0)
    def _():
        acc_s[...] = jnp.zeros(acc_s.shape, jnp.float32)

    a = jnp.maximum(jnp.dot(h_ref[...], w1_ref[...], preferred_element_type=jnp.float32), 0.0)
    acc_s[...] += jnp.dot((a * a).astype(jnp.bfloat16), w2_ref[...], preferred_element_type=jnp.float32)

    @pl.when(f == pl.num_programs(1) - 1)
    def _():
        x2 = x1_ref[...] + mod_ref[0:1, 5 * D:6 * D] * acc_s[...]
        y = x2 * lax.rsqrt(jnp.mean(x2 * x2, axis=-1, keepdims=True) + EPS)
        o_ref[...] = y * fg_ref[...]


def _ffn(h2, w1, w2, x1, mod, fg):
    n = h2.shape[0]
    row = lambda i, f: (i, 0)
    const = lambda i, f: (0, 0)
    return pl.pallas_call(
        _ffn_kernel,
        grid=(n // TM_FF, FF // TF),
        in_specs=[
            pl.BlockSpec((TM_FF, D), row),
            pl.BlockSpec((D, TF), lambda i, f: (0, f)),
            pl.BlockSpec((TF, D), lambda i, f: (f, 0)),
            pl.BlockSpec((TM_FF, D), row),
            pl.BlockSpec((2, 6 * D), const),
            pl.BlockSpec((1, D), const),
        ],
        out_specs=pl.BlockSpec((TM_FF, D), row),
        out_shape=jax.ShapeDtypeStruct((n, D), jnp.float32),
        scratch_shapes=[pltpu.VMEM((TM_FF, D), jnp.float32)],
        compiler_params=_params("parallel", "arbitrary"),
    )(h2, w1, w2, x1, mod, fg)


def _head_layout(w):
    w = w.reshape(D, N_HEADS, 2, HEAD_DIM // 2, 2)
    return w.transpose(0, 1, 4, 2, 3).reshape(D, A_W)


def _rope_tables(n_tokens):
    rows = n_tokens // GRID_W
    r, col = jnp.meshgrid(jnp.arange(rows, dtype=jnp.float32),
                          jnp.arange(GRID_W, dtype=jnp.float32), indexing="ij")
    n_freq = HEAD_DIM // 4
    inv = ROPE_BASE ** (-jnp.arange(n_freq, dtype=jnp.float32) / n_freq)
    ang = jnp.concatenate([r.reshape(-1, 1) * inv, col.reshape(-1, 1) * inv], axis=-1)
    cos, sin = jnp.cos(ang), jnp.sin(ang)
    return jnp.tile(cos, (1, 4)), jnp.concatenate([-sin, -sin, sin, sin], axis=-1)


def kernel(x, c, ctx, c_ctx, w_ada, b_ada, norm1_g, norm2_g, w_in, lam_q1, lam_k1, lam_q2, lam_k2, subln_g, sgu_ln_g, sgu_ln_b, w_spatial, b_spatial, w_gate, b_gate, w_br_attn, w_br_sgu, w_out, w_ff1, w_ff2, final_g):
    assert x.shape == (1, N_TOK, D) and ctx.shape == (1, CTX, D) and w_ada.shape[0] == 1
    bf = jnp.bfloat16
    x2d = x[0]
    ctx2d = ctx[0]

    ct = jnp.stack([c[0], c_ctx], axis=1)
    mod = _ada(ct, w_ada[0], b_ada[0][None, :])

    w_l = w_in[0]
    w_in_b = jnp.concatenate([_head_layout(w_l[:, :A_W]), _head_layout(w_l[:, A_W:2 * A_W]),
                              w_l[:, 2 * A_W:]], axis=1).astype(bf)
    cos_t, sin_t = _rope_tables(N_TOK)
    g1 = norm1_g[0][None, :]

    kc, vct = _ctx_proj(ctx2d, mod, g1, w_in_b)
    hx, q, k, vt, sx = _in_proj(x2d, mod, g1, w_in_b, cos_t, sin_t,
                                w_spatial[0].astype(bf), b_spatial[0].T,
                                sgu_ln_g[0][None, :], sgu_ln_b[0][None, :])
    ax = _attn(q, k, vt, kc, vct, lam_q1, lam_k1, lam_q2, lam_k2, subln_g[0][:, None])
    m = _merge(hx, ax, sx, w_gate[0].astype(bf), b_gate[0][None, :],
               w_br_attn[0].astype(bf), w_br_sgu[0].astype(bf))
    x1, h2 = _out_proj(m, w_out[0].astype(bf), x2d, mod, norm2_g[0][None, :])
    out = _ffn(h2, w_ff1[0].astype(bf), w_ff2[0].astype(bf), x1, mod, final_g[None, :])
    return out[None]
```

```python
import functools
import math

import jax
import jax.numpy as jnp
from jax import lax
from jax.experimental import pallas as pl
from jax.experimental.pallas import tpu as pltpu

D = 2048
N_TOK = 8192
GRID_W = 64
CTX = 256
N_HEADS = 8
HEAD_DIM = 64
PAIR = 2 * HEAD_DIM
A_W = N_HEADS * PAIR
SGU_W = D // 2
N_GROUPS = 8
GROUP_DIM = SGU_W // N_GROUPS
CHUNK = 128
FF = 4 * D
IN_W = 3 * A_W + 2 * SGU_W
ROPE_BASE = 10000.0
EPS = 1e-6
LAM_INIT = 0.8 - 0.6 * math.exp(-0.3 * 0)

V7X_VMEM_BYTES = 64 * 1024 * 1024
VMEM_LIMIT = V7X_VMEM_BYTES - 8 * 1024 * 1024
LANES = 128

TM_IN = 512
TQ = 256
TK = 384
TM_MERGE = 1024
TN_MERGE = 512
TM_OUT = 512
TM_FF = 512
TF = 1024
TN_ADA = 1024

NEG_BIG = -1e30


def _params(*sem):
    return pltpu.CompilerParams(dimension_semantics=sem, vmem_limit_bytes=VMEM_LIMIT)


def _gelu_tanh(x):
    return 0.5 * x * (1.0 + jnp.tanh(math.sqrt(2.0 / math.pi) * (x + 0.044715 * (x * x * x))))


def _mod_norm(xf, g, shift, scale):
    y = xf * lax.rsqrt(jnp.mean(xf * xf, axis=-1, keepdims=True) + EPS)
    return (y * g) * (1.0 + scale) + shift


def _ada_kernel(ct_ref, w_ref, b_ref, o_ref):
    w = w_ref[...]
    rows = []
    for r in range(2):
        cv = ct_ref[:, r:r + 1]
        a = cv * jax.nn.sigmoid(cv)
        rows.append(jnp.sum(a * w, axis=0, keepdims=True))
    o_ref[...] = jnp.concatenate(rows, axis=0) + b_ref[...]


def _ada(ct, w, b):
    n_out = w.shape[1]
    return pl.pallas_call(
        _ada_kernel,
        grid=(n_out // TN_ADA,),
        in_specs=[
            pl.BlockSpec((D, 2), lambda j: (0, 0)),
            pl.BlockSpec((D, TN_ADA), lambda j: (0, j)),
            pl.BlockSpec((1, TN_ADA), lambda j: (0, j)),
        ],
        out_specs=pl.BlockSpec((2, TN_ADA), lambda j: (0, j)),
        out_shape=jax.ShapeDtypeStruct((2, n_out), jnp.float32),
        compiler_params=_params("parallel"),
    )(ct, w, b)


def _ctx_kernel(x_ref, mod_ref, g_ref, w_ref, k_ref, vt_ref, h_s):
    j = pl.program_id(0)

    @pl.when(j == 0)
    def _():
        h = _mod_norm(x_ref[...], g_ref[...], mod_ref[1:2, 0:D], mod_ref[1:2, D:2 * D])
        h_s[...] = h.astype(jnp.bfloat16)

    acc = jnp.dot(h_s[...], w_ref[...], preferred_element_type=jnp.float32)

    @pl.when(j == 0)
    def _():
        k_ref[...] = acc.astype(jnp.bfloat16)

    @pl.when(j == 1)
    def _():
        vt_ref[...] = acc.T.astype(jnp.bfloat16)


def _ctx_proj(ctx2d, mod, g1, w_in):
    return pl.pallas_call(
        _ctx_kernel,
        grid=(2,),
        in_specs=[
            pl.BlockSpec((CTX, D), lambda j: (0, 0)),
            pl.BlockSpec((2, 6 * D), lambda j: (0, 0)),
            pl.BlockSpec((1, D), lambda j: (0, 0)),
            pl.BlockSpec((D, A_W), lambda j: (0, j + 1)),
        ],
        out_specs=[
            pl.BlockSpec((CTX, A_W), lambda j: (0, 0)),
            pl.BlockSpec((A_W, CTX), lambda j: (0, 0)),
        ],
        out_shape=[
            jax.ShapeDtypeStruct((CTX, A_W), jnp.bfloat16),
            jax.ShapeDtypeStruct((A_W, CTX), jnp.bfloat16),
        ],
        scratch_shapes=[pltpu.VMEM((CTX, D), jnp.bfloat16)],
        compiler_params=_params("arbitrary"),
    )(ctx2d, mod, g1, w_in)


def _rope(t, c, s):
    outs = []
    for h in range(N_HEADS):
        blk = t[:, h * PAIR:(h + 1) * PAIR]
        outs.append(blk * c + pltpu.roll(blk, PAIR // 2, axis=1) * s)
    return jnp.concatenate(outs, axis=1)


def _in_kernel(x_ref, mod_ref, g_ref, w_ref, cos_ref, sin_ref, ws_ref, bst_ref, lng_ref, lnb_ref,
               hx_ref, q_ref, k_ref, vt_ref, sx_ref, u_s):
    j = pl.program_id(1)

    @pl.when(j == 0)
    def _():
        h = _mod_norm(x_ref[...], g_ref[...], mod_ref[0:1, 0:D], mod_ref[0:1, D:2 * D])
        hx_ref[...] = h.astype(jnp.bfloat16)

    acc = jnp.dot(hx_ref[...], w_ref[...], preferred_element_type=jnp.float32)

    @pl.when(j == 0)
    def _():
        q_ref[...] = (_rope(acc, cos_ref[...], sin_ref[...]) * (HEAD_DIM ** -0.5)).astype(jnp.bfloat16)

    @pl.when(j == 1)
    def _():
        k_ref[...] = _rope(acc, cos_ref[...], sin_ref[...]).astype(jnp.bfloat16)

    @pl.when(j == 2)
    def _():
        vt_ref[...] = acc.T.astype(jnp.bfloat16)

    @pl.when(j == 3)
    def _():
        u_s[...] = _gelu_tanh(acc)

    @pl.when(j == 4)
    def _():
        v = _gelu_tanh(acc)
        mu = jnp.mean(v, axis=-1, keepdims=True)
        vc = v - mu
        var = jnp.mean(vc * vc, axis=-1, keepdims=True)
        vn = ((vc * lax.rsqrt(var + EPS)) * lng_ref[...] + lnb_ref[...]).astype(jnp.bfloat16)
        n_chunks = TM_IN // CHUNK
        for g in range(N_GROUPS):
            cols = slice(g * GROUP_DIM, (g + 1) * GROUP_DIM)
            rhs = jnp.concatenate([vn[c * CHUNK:(c + 1) * CHUNK, cols] for c in range(n_chunks)], axis=1)
            mixed = jnp.dot(ws_ref[g], rhs, preferred_element_type=jnp.float32) + bst_ref[:, g:g + 1]
            for c in range(n_chunks):
                rows = slice(c * CHUNK, (c + 1) * CHUNK)
                sx_ref[rows, cols] = (u_s[rows, cols] * mixed[:, c * GROUP_DIM:(c + 1) * GROUP_DIM]
                                      ).astype(jnp.bfloat16)


def _in_proj(x2d, mod, g1, w_in, cos_t, sin_t, ws, bst, lng, lnb):
    n = x2d.shape[0]
    row = lambda i, j: (i, 0)
    const = lambda i, j: (0, 0)
    return pl.pallas_call(
        _in_kernel,
        grid=(n // TM_IN, IN_W // A_W),
        in_specs=[
            pl.BlockSpec((TM_IN, D), row),
            pl.BlockSpec((2, 6 * D), const),
            pl.BlockSpec((1, D), const),
            pl.BlockSpec((D, A_W), lambda i, j: (0, j)),
            pl.BlockSpec((TM_IN, PAIR), row),
            pl.BlockSpec((TM_IN, PAIR), row),
            pl.BlockSpec((N_GROUPS, CHUNK, CHUNK), lambda i, j: (0, 0, 0)),
            pl.BlockSpec((CHUNK, N_GROUPS), const),
            pl.BlockSpec((1, SGU_W), const),
            pl.BlockSpec((1, SGU_W), const),
        ],
        out_specs=[
            pl.BlockSpec((TM_IN, D), row),
            pl.BlockSpec((TM_IN, A_W), row),
            pl.BlockSpec((TM_IN, A_W), row),
            pl.BlockSpec((A_W, TM_IN), lambda i, j: (0, i)),
            pl.BlockSpec((TM_IN, SGU_W), row),
        ],
        out_shape=[
            jax.ShapeDtypeStruct((n, D), jnp.bfloat16),
            jax.ShapeDtypeStruct((n, A_W), jnp.bfloat16),
            jax.ShapeDtypeStruct((n, A_W), jnp.bfloat16),
            jax.ShapeDtypeStruct((A_W, n), jnp.bfloat16),
            jax.ShapeDtypeStruct((n, SGU_W), jnp.bfloat16),
        ],
        scratch_shapes=[pltpu.VMEM((TM_IN, SGU_W), jnp.float32)],
        compiler_params=_params("parallel", "arbitrary"),
    )(x2d, mod, g1, w_in, cos_t, sin_t, ws, bst, lng, lnb)


def _attn_kernel(q_ref, k_ref, vt_ref, kc_ref, vct_ref, lq1_ref, lk1_ref, lq2_ref, lk2_ref, sg_ref,
                 o_ref, kk_s, vv_s, s_s, bm_s, p_s, acc_s):
    n_lat = k_ref.shape[0]

    @pl.when(pl.program_id(1) == 0)
    def _():
        kk_s[0:n_lat, :] = k_ref[...]
        kk_s[n_lat:, :] = kc_ref[...]
        vv_s[:, 0:n_lat] = vt_ref[...]
        vv_s[:, n_lat:] = vct_ref[...]

    q = q_ref[...]
    lane = lax.broadcasted_iota(jnp.int32, q.shape, 1)
    first = (lane % HEAD_DIM) < (HEAD_DIM // 2)
    zero = jnp.zeros_like(q)
    qq = jnp.concatenate([jnp.where(first, q, zero), jnp.where(first, zero, q)], axis=0)

    def scores(t, slot):
        start = pl.multiple_of(t * TK, TK)
        s = lax.dot_general(kk_s[pl.ds(start, TK), :], qq, (((1,), (1,)), ((), ())),
                            preferred_element_type=jnp.float32)
        s_s[slot] = s
        bm_s[slot] = jnp.max(s, axis=0, keepdims=True)

    def softmax(slot, m, l):
        m_new = jnp.maximum(m, bm_s[slot])
        alpha = jnp.exp(m - m_new)
        p = jnp.exp(s_s[slot] - m_new)
        p_s[slot] = p.astype(jnp.bfloat16)
        return m_new, alpha * l + jnp.sum(p, axis=0, keepdims=True), alpha

    def values(t, slot, alpha):
        start = pl.multiple_of(t * TK, TK)
        acc_s[...] = alpha * acc_s[...] + jnp.dot(vv_s[:, pl.ds(start, TK)], p_s[slot],
                                                  preferred_element_type=jnp.float32)

    n_blocks = kk_s.shape[0] // TK
    stat = (1, 2 * TQ)
    acc_s[...] = jnp.zeros(acc_s.shape, jnp.float32)
    scores(0, 0)
    scores(1, 1)
    m, l, alpha = softmax(0, jnp.full(stat, NEG_BIG, jnp.float32), jnp.zeros(stat, jnp.float32))

    def step(t, cur, carry):
        m, l, alpha_prev = carry
        scores(t + 1, 1 - cur)
        m, l, alpha = softmax(cur, m, l)
        values(t - 1, 1 - cur, alpha_prev)
        return m, l, alpha

    def body(u, carry):
        t = 1 + 2 * u
        return step(t + 1, 0, step(t, 1, carry))

    m, l, alpha = lax.fori_loop(0, (n_blocks - 2) // 2, body, (m, l, alpha))
    last = (n_blocks - 1) % 2
    m, l, alpha_last = softmax(last, m, l)
    values(n_blocks - 2, 1 - last, alpha)
    values(n_blocks - 1, last, alpha_last)

    lam = (jnp.exp(jnp.sum(lq1_ref[...] * lk1_ref[...], axis=-1, keepdims=True))
           - jnp.exp(jnp.sum(lq2_ref[...] * lk2_ref[...], axis=-1, keepdims=True)) + LAM_INIT)
    o_both = acc_s[...] * (1.0 / l)
    o = o_both[:, :TQ] - lam * o_both[:, TQ:]
    y = o * lax.rsqrt(jnp.mean(o * o, axis=0, keepdims=True) + EPS)
    y = (y * sg_ref[...]) * (1.0 - LAM_INIT)
    o_ref[...] = y.T.astype(jnp.bfloat16)


def _attn(q, k, vt, kc, vct, lq1, lk1, lq2, lk2, sg_col):
    n = q.shape[0]
    n_keys = n + CTX
    assert n_keys % TK == 0 and n_keys // TK >= 4 and (n_keys // TK) % 2 == 0
    vec = pl.BlockSpec((1, HEAD_DIM), lambda h, i: (0, 0))
    return pl.pallas_call(
        _attn_kernel,
        grid=(N_HEADS, n // TQ),
        in_specs=[
            pl.BlockSpec((TQ, PAIR), lambda h, i: (i, h)),
            pl.BlockSpec((n, PAIR), lambda h, i: (0, h)),
            pl.BlockSpec((PAIR, n), lambda h, i: (h, 0)),
            pl.BlockSpec((CTX, PAIR), lambda h, i: (0, h)),
            pl.BlockSpec((PAIR, CTX), lambda h, i: (h, 0)),
            vec, vec, vec, vec,
            pl.BlockSpec((PAIR, 1), lambda h, i: (0, 0)),
        ],
        out_specs=pl.BlockSpec((TQ, PAIR), lambda h, i: (i, h)),
        out_shape=jax.ShapeDtypeStruct((n, A_W), jnp.bfloat16),
        scratch_shapes=[
            pltpu.VMEM((n_keys, PAIR), jnp.bfloat16),
            pltpu.VMEM((PAIR, n_keys), jnp.bfloat16),
            pltpu.VMEM((2, TK, 2 * TQ), jnp.float32),
            pltpu.VMEM((2, 1, 2 * TQ), jnp.float32),
            pltpu.VMEM((2, TK, 2 * TQ), jnp.bfloat16),
            pltpu.VMEM((PAIR, 2 * TQ), jnp.float32),
        ],
        compiler_params=_params("arbitrary", "arbitrary"),
    )(q, k, vt, kc, vct, lq1, lk1, lq2, lk2, sg_col)


def _merge_kernel(hx_ref, ax_ref, sx_ref, wga_ref, wgs_ref, bga_ref, bgs_ref, wba_ref, wbs_ref, m_ref):
    hx = hx_ref[...]
    ga = jax.nn.sigmoid(jnp.dot(hx, wga_ref[...], preferred_element_type=jnp.float32) + bga_ref[...])
    gs = jax.nn.sigmoid(jnp.dot(hx, wgs_ref[...], preferred_element_type=jnp.float32) + bgs_ref[...])
    ba = jnp.dot(ax_ref[...], wba_ref[...], preferred_element_type=jnp.float32)
    bs = jnp.dot(sx_ref[...], wbs_ref[...], preferred_element_type=jnp.float32)
    m_ref[...] = (ga * ba + gs * bs).astype(jnp.bfloat16)


def _merge(hx, ax, sx, w_gate, b_gate, w_ba, w_bs):
    n = hx.shape[0]
    nj = D // TN_MERGE
    row = lambda i, j: (i, 0)
    return pl.pallas_call(
        _merge_kernel,
        grid=(n // TM_MERGE, nj),
        in_specs=[
            pl.BlockSpec((TM_MERGE, D), row),
            pl.BlockSpec((TM_MERGE, A_W), row),
            pl.BlockSpec((TM_MERGE, SGU_W), row),
            pl.BlockSpec((D, TN_MERGE), lambda i, j: (0, j)),
            pl.BlockSpec((D, TN_MERGE), lambda i, j: (0, j + nj)),
            pl.BlockSpec((1, TN_MERGE), lambda i, j: (0, j)),
            pl.BlockSpec((1, TN_MERGE), lambda i, j: (0, j + nj)),
            pl.BlockSpec((A_W, TN_MERGE), lambda i, j: (0, j)),
            pl.BlockSpec((SGU_W, TN_MERGE), lambda i, j: (0, j)),
        ],
        out_specs=pl.BlockSpec((TM_MERGE, TN_MERGE), lambda i, j: (i, j)),
        out_shape=jax.ShapeDtypeStruct((n, D), jnp.bfloat16),
        compiler_params=_params("parallel", "arbitrary"),
    )(hx, ax, sx, w_gate, w_gate, b_gate, b_gate, w_ba, w_bs)


def _out_kernel(m_ref, w_ref, x_ref, mod_ref, g_ref, x1_ref, h2_ref):
    y = jnp.dot(m_ref[...], w_ref[...], preferred_element_type=jnp.float32)
    x1 = x_ref[...] + mod_ref[0:1, 2 * D:3 * D] * y
    x1_ref[...] = x1
    h2 = _mod_norm(x1, g_ref[...], mod_ref[0:1, 3 * D:4 * D], mod_ref[0:1, 4 * D:5 * D])
    h2_ref[...] = h2.astype(jnp.bfloat16)


def _out_proj(m, w_out, x2d, mod, g2):
    n = m.shape[0]
    row = lambda i: (i, 0)
    const = lambda i: (0, 0)
    return pl.pallas_call(
        _out_kernel,
        grid=(n // TM_OUT,),
        in_specs=[
            pl.BlockSpec((TM_OUT, D), row),
            pl.BlockSpec((D, D), const),
            pl.BlockSpec((TM_OUT, D), row),
            pl.BlockSpec((2, 6 * D), const),
            pl.BlockSpec((1, D), const),
        ],
        out_specs=[pl.BlockSpec((TM_OUT, D), row), pl.BlockSpec((TM_OUT, D), row)],
        out_shape=[jax.ShapeDtypeStruct((n, D), jnp.float32), jax.ShapeDtypeStruct((n, D), jnp.bfloat16)],
        compiler_params=_params("parallel"),
    )(m, w_out, x2d, mod, g2)


def _ffn_kernel(h_ref, w1_ref, w2_ref, x1_ref, mod_ref, fg_ref, o_ref, acc_s):
    f = pl.program_id(1)

    @pl.when(f == 0)
    def _():
        acc_s[...] = jnp.zeros(acc_s.shape, jnp.float32)

    a = jnp.maximum(jnp.dot(h_ref[...], w1_ref[...], preferred_element_type=jnp.float32), 0.0)
    acc_s[...] += jnp.dot((a * a).astype(jnp.bfloat16), w2_ref[...], preferred_element_type=jnp.float32)

    @pl.when(f == pl.num_programs(1) - 1)
    def _():
        x2 = x1_ref[...] + mod_ref[0:1, 5 * D:6 * D] * acc_s[...]
        y = x2 * lax.rsqrt(jnp.mean(x2 * x2, axis=-1, keepdims=True) + EPS)
        o_ref[...] = y * fg_ref[...]


def _ffn(h2, w1, w2, x1, mod, fg):
    n = h2.shape[0]
    row = lambda i, f: (i, 0)
    const = lambda i, f: (0, 0)
    return pl.pallas_call(
        _ffn_kernel,
        grid=(n // TM_FF, FF // TF),
        in_specs=[
            pl.BlockSpec((TM_FF, D), row),
            pl.BlockSpec((D, TF), lambda i, f: (0, f)),
            pl.BlockSpec((TF, D), lambda i, f: (f, 0)),
            pl.BlockSpec((TM_FF, D), row),
            pl.BlockSpec((2, 6 * D), const),
            pl.BlockSpec((1, D), const),
        ],
        out_specs=pl.BlockSpec((TM_FF, D), row),
        out_shape=jax.ShapeDtypeStruct((n, D), jnp.float32),
        scratch_shapes=[pltpu.VMEM((TM_FF, D), jnp.float32)],
        compiler_params=_params("parallel", "arbitrary"),
    )(h2, w1, w2, x1, mod, fg)


def _head_layout(w):
    w = w.reshape(D, N_HEADS, 2, HEAD_DIM // 2, 2)
    return w.transpose(0, 1, 4, 2, 3).reshape(D, A_W)


def _rope_tables(n_tokens):
    rows = n_tokens // GRID_W
    r, col = jnp.meshgrid(jnp.arange(rows, dtype=jnp.float32),
                          jnp.arange(GRID_W, dtype=jnp.float32), indexing="ij")
    n_freq = HEAD_DIM // 4
    inv = ROPE_BASE ** (-jnp.arange(n_freq, dtype=jnp.float32) / n_freq)
    ang = jnp.concatenate([r.reshape(-1, 1) * inv, col.reshape(-1, 1) * inv], axis=-1)
    cos, sin = jnp.cos(ang), jnp.sin(ang)
    return jnp.tile(cos, (1, 4)), jnp.concatenate([-sin, -sin, sin, sin], axis=-1)


def kernel(x, c, ctx, c_ctx, w_ada, b_ada, norm1_g, norm2_g, w_in, lam_q1, lam_k1, lam_q2, lam_k2, subln_g, sgu_ln_g, sgu_ln_b, w_spatial, b_spatial, w_gate, b_gate, w_br_attn, w_br_sgu, w_out, w_ff1, w_ff2, final_g):
    assert x.shape == (1, N_TOK, D) and ctx.shape == (1, CTX, D) and w_ada.shape[0] == 1
    bf = jnp.bfloat16
    x2d = x[0]
    ctx2d = ctx[0]

    ct = jnp.stack([c[0], c_ctx], axis=1)
    mod = _ada(ct, w_ada[0], b_ada[0][None, :])

    w_l = w_in[0]
    w_in_b = jnp.concatenate([_head_layout(w_l[:, :A_W]), _head_layout(w_l[:, A_W:2 * A_W]),
                              w_l[:, 2 * A_W:]], axis=1).astype(bf)
    cos_t, sin_t = _rope_tables(N_TOK)
    g1 = norm1_g[0][None, :]

    kc, vct = _ctx_proj(ctx2d, mod, g1, w_in_b)
    hx, q, k, vt, sx = _in_proj(x2d, mod, g1, w_in_b, cos_t, sin_t,
                                w_spatial[0].astype(bf), b_spatial[0].T,
                                sgu_ln_g[0][None, :], sgu_ln_b[0][None, :])
    ax = _attn(q, k, vt, kc, vct, lam_q1, lam_k1, lam_q2, lam_k2, subln_g[0][:, None])
    m = _merge(hx, ax, sx, w_gate[0].astype(bf), b_gate[0][None, :],
               w_br_attn[0].astype(bf), w_br_sgu[0].astype(bf))
    x1, h2 = _out_proj(m, w_out[0].astype(bf), x2d, mod, norm2_g[0][None, :])
    out = _ffn(h2, w_ff1[0].astype(bf), w_ff2[0].astype(bf), x1, mod, final_g[None, :])
    return out[None]
```

```python
import functools
import math

import jax
import jax.numpy as jnp
from jax import lax
from jax.experimental import pallas as pl
from jax.experimental.pallas import tpu as pltpu

D = 2048
N_TOK = 8192
GRID_W = 64
CTX = 256
N_HEADS = 8
HEAD_DIM = 64
PAIR = 2 * HEAD_DIM
A_W = N_HEADS * PAIR
SGU_W = D // 2
N_GROUPS = 8
GROUP_DIM = SGU_W // N_GROUPS
CHUNK = 128
FF = 4 * D
IN_W = 3 * A_W + 2 * SGU_W
ROPE_BASE = 10000.0
EPS = 1e-6
LAM_INIT = 0.8 - 0.6 * math.exp(-0.3 * 0)
Q_SCALE = HEAD_DIM ** -0.5 * math.log2(math.e)

V7X_VMEM_BYTES = 64 * 1024 * 1024
VMEM_LIMIT = V7X_VMEM_BYTES - 8 * 1024 * 1024
LANES = 128

TM_IN = 512
TQ = 256
TK = 768
N_SLOTS = 3
TM_MERGE = 1024
TN_MERGE = 512
TM_OUT = 512
TM_FF = 512
TF = 1024
TN_ADA = 1024

NEG_BIG = -1e30


def _params(*sem):
    return pltpu.CompilerParams(dimension_semantics=sem, vmem_limit_bytes=VMEM_LIMIT)


def _gelu_tanh(x):
    return 0.5 * x * (1.0 + jnp.tanh(math.sqrt(2.0 / math.pi) * (x + 0.044715 * (x * x * x))))


def _mod_norm(xf, g, shift, scale):
    y = xf * lax.rsqrt(jnp.mean(xf * xf, axis=-1, keepdims=True) + EPS)
    return (y * g) * (1.0 + scale) + shift


def _ada_kernel(ct_ref, w_ref, b_ref, o_ref):
    w = w_ref[...]
    rows = []
    for r in range(2):
        cv = ct_ref[:, r:r + 1]
        a = cv * jax.nn.sigmoid(cv)
        rows.append(jnp.sum(a * w, axis=0, keepdims=True))
    o_ref[...] = jnp.concatenate(rows, axis=0) + b_ref[...]


def _ada(ct, w, b):
    n_out = w.shape[1]
    return pl.pallas_call(
        _ada_kernel,
        grid=(n_out // TN_ADA,),
        in_specs=[
            pl.BlockSpec((D, 2), lambda j: (0, 0)),
            pl.BlockSpec((D, TN_ADA), lambda j: (0, j)),
            pl.BlockSpec((1, TN_ADA), lambda j: (0, j)),
        ],
        out_specs=pl.BlockSpec((2, TN_ADA), lambda j: (0, j)),
        out_shape=jax.ShapeDtypeStruct((2, n_out), jnp.float32),
        compiler_params=_params("parallel"),
    )(ct, w, b)


def _ctx_kernel(x_ref, mod_ref, g_ref, w_ref, k_ref, vt_ref, h_s):
    j = pl.program_id(0)

    @pl.when(j == 0)
    def _():
        h = _mod_norm(x_ref[...], g_ref[...], mod_ref[1:2, 0:D], mod_ref[1:2, D:2 * D])
        h_s[...] = h.astype(jnp.bfloat16)

    acc = jnp.dot(h_s[...], w_ref[...], preferred_element_type=jnp.float32)

    @pl.when(j == 0)
    def _():
        k_ref[...] = acc.astype(jnp.bfloat16)

    @pl.when(j == 1)
    def _():
        vt_ref[...] = acc.T.astype(jnp.bfloat16)


def _ctx_proj(ctx2d, mod, g1, w_in):
    return pl.pallas_call(
        _ctx_kernel,
        grid=(2,),
        in_specs=[
            pl.BlockSpec((CTX, D), lambda j: (0, 0)),
            pl.BlockSpec((2, 6 * D), lambda j: (0, 0)),
            pl.BlockSpec((1, D), lambda j: (0, 0)),
            pl.BlockSpec((D, A_W), lambda j: (0, j + 1)),
        ],
        out_specs=[
            pl.BlockSpec((CTX, A_W), lambda j: (0, 0)),
            pl.BlockSpec((A_W, CTX), lambda j: (0, 0)),
        ],
        out_shape=[
            jax.ShapeDtypeStruct((CTX, A_W), jnp.bfloat16),
            jax.ShapeDtypeStruct((A_W, CTX), jnp.bfloat16),
        ],
        scratch_shapes=[pltpu.VMEM((CTX, D), jnp.bfloat16)],
        compiler_params=_params("arbitrary"),
    )(ctx2d, mod, g1, w_in)


def _rope(t, c, s):
    outs = []
    for h in range(N_HEADS):
        blk = t[:, h * PAIR:(h + 1) * PAIR]
        outs.append(blk * c + pltpu.roll(blk, PAIR // 2, axis=1) * s)
    return jnp.concatenate(outs, axis=1)


def _in_kernel(x_ref, mod_ref, g_ref, w_ref, cos_ref, sin_ref, ws_ref, bst_ref, lng_ref, lnb_ref,
               hx_ref, q_ref, k_ref, vt_ref, sx_ref, u_s):
    j = pl.program_id(1)

    @pl.when(j == 0)
    def _():
        h = _mod_norm(x_ref[...], g_ref[...], mod_ref[0:1, 0:D], mod_ref[0:1, D:2 * D])
        hx_ref[...] = h.astype(jnp.bfloat16)

    acc = jnp.dot(hx_ref[...], w_ref[...], preferred_element_type=jnp.float32)

    @pl.when(j == 0)
    def _():
        q_ref[...] = (_rope(acc, cos_ref[...], sin_ref[...]) * Q_SCALE).astype(jnp.bfloat16)

    @pl.when(j == 1)
    def _():
        k_ref[...] = _rope(acc, cos_ref[...], sin_ref[...]).astype(jnp.bfloat16)

    @pl.when(j == 2)
    def _():
        vt_ref[...] = acc.T.astype(jnp.bfloat16)

    @pl.when(j == 3)
    def _():
        u_s[...] = _gelu_tanh(acc)

    @pl.when(j == 4)
    def _():
        v = _gelu_tanh(acc)
        mu = jnp.mean(v, axis=-1, keepdims=True)
        vc = v - mu
        var = jnp.mean(vc * vc, axis=-1, keepdims=True)
        vn = ((vc * lax.rsqrt(var + EPS)) * lng_ref[...] + lnb_ref[...]).astype(jnp.bfloat16)
        n_chunks = TM_IN // CHUNK
        for g in range(N_GROUPS):
            cols = slice(g * GROUP_DIM, (g + 1) * GROUP_DIM)
            rhs = jnp.concatenate([vn[c * CHUNK:(c + 1) * CHUNK, cols] for c in range(n_chunks)], axis=1)
            mixed = jnp.dot(ws_ref[g], rhs, preferred_element_type=jnp.float32) + bst_ref[:, g:g + 1]
            for c in range(n_chunks):
                rows = slice(c * CHUNK, (c + 1) * CHUNK)
                sx_ref[rows, cols] = (u_s[rows, cols] * mixed[:, c * GROUP_DIM:(c + 1) * GROUP_DIM]
                                      ).astype(jnp.bfloat16)


def _in_proj(x2d, mod, g1, w_in, cos_t, sin_t, ws, bst, lng, lnb):
    n = x2d.shape[0]
    row = lambda i, j: (i, 0)
    const = lambda i, j: (0, 0)
    return pl.pallas_call(
        _in_kernel,
        grid=(n // TM_IN, IN_W // A_W),
        in_specs=[
            pl.BlockSpec((TM_IN, D), row),
            pl.BlockSpec((2, 6 * D), const),
            pl.BlockSpec((1, D), const),
            pl.BlockSpec((D, A_W), lambda i, j: (0, j)),
            pl.BlockSpec((TM_IN, PAIR), row),
            pl.BlockSpec((TM_IN, PAIR), row),
            pl.BlockSpec((N_GROUPS, CHUNK, CHUNK), lambda i, j: (0, 0, 0)),
            pl.BlockSpec((CHUNK, N_GROUPS), const),
            pl.BlockSpec((1, SGU_W), const),
            pl.BlockSpec((1, SGU_W), const),
        ],
        out_specs=[
            pl.BlockSpec((TM_IN, D), row),
            pl.BlockSpec((TM_IN, A_W), row),
            pl.BlockSpec((TM_IN, A_W), row),
            pl.BlockSpec((A_W, TM_IN), lambda i, j: (0, i)),
            pl.BlockSpec((TM_IN, SGU_W), row),
        ],
        out_shape=[
            jax.ShapeDtypeStruct((n, D), jnp.bfloat16),
            jax.ShapeDtypeStruct((n, A_W), jnp.bfloat16),
            jax.ShapeDtypeStruct((n, A_W), jnp.bfloat16),
            jax.ShapeDtypeStruct((A_W, n), jnp.bfloat16),
            jax.ShapeDtypeStruct((n, SGU_W), jnp.bfloat16),
        ],
        scratch_shapes=[pltpu.VMEM((TM_IN, SGU_W), jnp.float32)],
        compiler_params=_params("parallel", "arbitrary"),
    )(x2d, mod, g1, w_in, cos_t, sin_t, ws, bst, lng, lnb)


def _attn_kernel(q_ref, k_ref, vt_ref, kc_ref, vct_ref, lq1_ref, lk1_ref, lq2_ref, lk2_ref, sg_ref,
                 o_ref, kk_s, vv_s, s_s, bm_s, p_s, acc_s):
    n_lat = k_ref.shape[0]

    @pl.when(pl.program_id(1) == 0)
    def _():
        kk_s[0:n_lat, :] = k_ref[...]
        kk_s[n_lat:, :] = kc_ref[...]
        vv_s[:, 0:n_lat] = vt_ref[...]
        vv_s[:, n_lat:] = vct_ref[...]

    q = q_ref[...]
    lane = lax.broadcasted_iota(jnp.int32, q.shape, 1)
    first = (lane % HEAD_DIM) < (HEAD_DIM // 2)
    zero = jnp.zeros_like(q)
    qq = jnp.concatenate([jnp.where(first, q, zero), jnp.where(first, zero, q)], axis=0)

    def scores(t, slot):
        start = pl.multiple_of(t * TK, TK)
        s = lax.dot_general(kk_s[pl.ds(start, TK), :], qq, (((1,), (1,)), ((), ())),
                            preferred_element_type=jnp.float32)
        s_s[slot] = s
        bm_s[slot] = jnp.max(s, axis=0, keepdims=True)

    def softmax(slot, m, l):
        m_new = jnp.maximum(m, bm_s[slot])
        alpha = jnp.exp2(m - m_new)
        p = jnp.exp2(s_s[slot] - m_new)
        p_s[slot] = p.astype(jnp.bfloat16)
        return m_new, alpha * l + jnp.sum(p, axis=0, keepdims=True), alpha

    def values(t, slot, alpha):
        start = pl.multiple_of(t * TK, TK)
        acc_s[...] = alpha * acc_s[...] + jnp.dot(vv_s[:, pl.ds(start, TK)], p_s[slot],
                                                  preferred_element_type=jnp.float32)

    n_blocks = kk_s.shape[0] // TK
    stat = (1, 2 * TQ)
    acc_s[...] = jnp.zeros(acc_s.shape, jnp.float32)
    scores(0, 0)
    scores(1, 1)
    carry = softmax(0, jnp.full(stat, NEG_BIG, jnp.float32), jnp.zeros(stat, jnp.float32))

    def step(t, cur, carry):
        m, l, alpha_prev = carry
        scores(t + 1, (cur + 1) % N_SLOTS)
        m, l, alpha = softmax(cur, m, l)
        values(t - 1, (cur - 1) % N_SLOTS, alpha_prev)
        return m, l, alpha

    def body(u, carry):
        t0 = 1 + N_SLOTS * u
        for j in range(N_SLOTS):
            carry = step(t0 + j, (1 + j) % N_SLOTS, carry)
        return carry

    n_trips = (n_blocks - 2) // N_SLOTS
    carry = lax.fori_loop(0, n_trips, body, carry)
    for t in range(1 + N_SLOTS * n_trips, n_blocks - 1):
        carry = step(t, t % N_SLOTS, carry)
    m, l, alpha = carry
    last = n_blocks - 1
    m, l, alpha_last = softmax(last % N_SLOTS, m, l)
    values(last - 1, (last - 1) % N_SLOTS, alpha)
    values(last, last % N_SLOTS, alpha_last)

    lam = (jnp.exp(jnp.sum(lq1_ref[...] * lk1_ref[...], axis=-1, keepdims=True))
           - jnp.exp(jnp.sum(lq2_ref[...] * lk2_ref[...], axis=-1, keepdims=True)) + LAM_INIT)
    o_both = acc_s[...] * (1.0 / l)
    o = o_both[:, :TQ] - lam * o_both[:, TQ:]
    y = o * lax.rsqrt(jnp.mean(o * o, axis=0, keepdims=True) + EPS)
    y = (y * sg_ref[...]) * (1.0 - LAM_INIT)
    o_ref[...] = y.T.astype(jnp.bfloat16)


def _attn(q, k, vt, kc, vct, lq1, lk1, lq2, lk2, sg_col):
    n = q.shape[0]
    n_keys = n + CTX
    assert n_keys % TK == 0 and n_keys // TK >= 3
    vec = pl.BlockSpec((1, HEAD_DIM), lambda h, i: (0, 0))
    return pl.pallas_call(
        _attn_kernel,
        grid=(N_HEADS, n // TQ),
        in_specs=[
            pl.BlockSpec((TQ, PAIR), lambda h, i: (i, h)),
            pl.BlockSpec((n, PAIR), lambda h, i: (0, h)),
            pl.BlockSpec((PAIR, n), lambda h, i: (h, 0)),
            pl.BlockSpec((CTX, PAIR), lambda h, i: (0, h)),
            pl.BlockSpec((PAIR, CTX), lambda h, i: (h, 0)),
            vec, vec, vec, vec,
            pl.BlockSpec((PAIR, 1), lambda h, i: (0, 0)),
        ],
        out_specs=pl.BlockSpec((TQ, PAIR), lambda h, i: (i, h)),
        out_shape=jax.ShapeDtypeStruct((n, A_W), jnp.bfloat16),
        scratch_shapes=[
            pltpu.VMEM((n_keys, PAIR), jnp.bfloat16),
            pltpu.VMEM((PAIR, n_keys), jnp.bfloat16),
            pltpu.VMEM((N_SLOTS, TK, 2 * TQ), jnp.float32),
            pltpu.VMEM((N_SLOTS, 1, 2 * TQ), jnp.float32),
            pltpu.VMEM((N_SLOTS, TK, 2 * TQ), jnp.bfloat16),
            pltpu.VMEM((PAIR, 2 * TQ), jnp.float32),
        ],
        compiler_params=_params("arbitrary", "arbitrary"),
    )(q, k, vt, kc, vct, lq1, lk1, lq2, lk2, sg_col)


def _merge_kernel(hx_ref, ax_ref, sx_ref, wga_ref, wgs_ref, bga_ref, bgs_ref, wba_ref, wbs_ref, m_ref):
    hx = hx_ref[...]
    ga = jax.nn.sigmoid(jnp.dot(hx, wga_ref[...], preferred_element_type=jnp.float32) + bga_ref[...])
    gs = jax.nn.sigmoid(jnp.dot(hx, wgs_ref[...], preferred_element_type=jnp.float32) + bgs_ref[...])
    ba = jnp.dot(ax_ref[...], wba_ref[...], preferred_element_type=jnp.float32)
    bs = jnp.dot(sx_ref[...], wbs_ref[...], preferred_element_type=jnp.float32)
    m_ref[...] = (ga * ba + gs * bs).astype(jnp.bfloat16)


def _merge(hx, ax, sx, w_gate, b_gate, w_ba, w_bs):
    n = hx.shape[0]
    nj = D // TN_MERGE
    row = lambda i, j: (i, 0)
    return pl.pallas_call(
        _merge_kernel,
        grid=(n // TM_MERGE, nj),
        in_specs=[
            pl.BlockSpec((TM_MERGE, D), row),
            pl.BlockSpec((TM_MERGE, A_W), row),
            pl.BlockSpec((TM_MERGE, SGU_W), row),
            pl.BlockSpec((D, TN_MERGE), lambda i, j: (0, j)),
            pl.BlockSpec((D, TN_MERGE), lambda i, j: (0, j + nj)),
            pl.BlockSpec((1, TN_MERGE), lambda i, j: (0, j)),
            pl.BlockSpec((1, TN_MERGE), lambda i, j: (0, j + nj)),
            pl.BlockSpec((A_W, TN_MERGE), lambda i, j: (0, j)),
            pl.BlockSpec((SGU_W, TN_MERGE), lambda i, j: (0, j)),
        ],
        out_specs=pl.BlockSpec((TM_MERGE, TN_MERGE), lambda i, j: (i, j)),
        out_shape=jax.ShapeDtypeStruct((n, D), jnp.bfloat16),
        compiler_params=_params("parallel", "arbitrary"),
    )(hx, ax, sx, w_gate, w_gate, b_gate, b_gate, w_ba, w_bs)


def _out_kernel(m_ref, w_ref, x_ref, mod_ref, g_ref, x1_ref, h2_ref):
    y = jnp.dot(m_ref[...], w_ref[...], preferred_element_type=jnp.float32)
    x1 = x_ref[...] + mod_ref[0:1, 2 * D:3 * D] * y
    x1_ref[...] = x1
    h2 = _mod_norm(x1, g_ref[...], mod_ref[0:1, 3 * D:4 * D], mod_ref[0:1, 4 * D:5 * D])
    h2_ref[...] = h2.astype(jnp.bfloat16)


def _out_proj(m, w_out, x2d, mod, g2):
    n = m.shape[0]
    row = lambda i: (i, 0)
    const = lambda i: (0, 0)
    return pl.pallas_call(
        _out_kernel,
        grid=(n // TM_OUT,),
        in_specs=[
            pl.BlockSpec((TM_OUT, D), row),
            pl.BlockSpec((D, D), const),
            pl.BlockSpec((TM_OUT, D), row),
            pl.BlockSpec((2, 6 * D), const),
            pl.BlockSpec((1, D), const),
        ],
        out_specs=[pl.BlockSpec((TM_OUT, D), row), pl.BlockSpec((TM_OUT, D), row)],
        out_shape=[jax.ShapeDtypeStruct((n, D), jnp.float32), jax.ShapeDtypeStruct((n, D), jnp.bfloat16)],
        compiler_params=_params("parallel"),
    )(m, w_out, x2d, mod, g2)


def _ffn_kernel(h_ref, w1_ref, w2_ref, x1_ref, mod_ref, fg_ref, o_ref, acc_s):
    f = pl.program_id(1)

    @pl.when(f == 0)
    def _():
        acc_s[...] = jnp.zeros(acc_s.shape, jnp.float32)

    a = jnp.maximum(jnp.dot(h_ref[...], w1_ref[...], preferred_element_type=jnp.float32), 0.0)
    acc_s[...] += jnp.dot((a * a).astype(jnp.bfloat16), w2_ref[...], preferred_element_type=jnp.float32)

    @pl.when(f == pl.num_programs(1) - 1)
    def _():
        x2 = x1_ref[...] + mod_ref[0:1, 5 * D:6 * D] * acc_s[...]
        y = x2 * lax.rsqrt(jnp.mean(x2 * x2, axis=-1, keepdims=True) + EPS)
        o_ref[...] = y * fg_ref[...]


def _ffn(h2, w1, w2, x1, mod, fg):
    n = h2.shape[0]
    row = lambda i, f: (i, 0)
    const = lambda i, f: (0, 0)
    return pl.pallas_call(
        _ffn_kernel,
        grid=(n // TM_FF, FF // TF),
        in_specs=[
            pl.BlockSpec((TM_FF, D), row),
            pl.BlockSpec((D, TF), lambda i, f: (0, f)),
            pl.BlockSpec((TF, D), lambda i, f: (f, 0)),
            pl.BlockSpec((TM_FF, D), row),
            pl.BlockSpec((2, 6 * D), const),
            pl.BlockSpec((1, D), const),
        ],
        out_specs=pl.BlockSpec((TM_FF, D), row),
        out_shape=jax.ShapeDtypeStruct((n, D), jnp.float32),
        scratch_shapes=[pltpu.VMEM((TM_FF, D), jnp.float32)],
        compiler_params=_params("parallel", "arbitrary"),
    )(h2, w1, w2, x1, mod, fg)


def _head_layout(w):
    w = w.reshape(D, N_HEADS, 2, HEAD_DIM // 2, 2)
    return w.transpose(0, 1, 4, 2, 3).reshape(D, A_W)


def _rope_tables(n_tokens):
    rows = n_tokens // GRID_W
    r, col = jnp.meshgrid(jnp.arange(rows, dtype=jnp.float32),
                          jnp.arange(GRID_W, dtype=jnp.float32), indexing="ij")
    n_freq = HEAD_DIM // 4
    inv = ROPE_BASE ** (-jnp.arange(n_freq, dtype=jnp.float32) / n_freq)
    ang = jnp.concatenate([r.reshape(-1, 1) * inv, col.reshape(-1, 1) * inv], axis=-1)
    cos, sin = jnp.cos(ang), jnp.sin(ang)
    return jnp.tile(cos, (1, 4)), jnp.concatenate([-sin, -sin, sin, sin], axis=-1)


def kernel(x, c, ctx, c_ctx, w_ada, b_ada, norm1_g, norm2_g, w_in, lam_q1, lam_k1, lam_q2, lam_k2, subln_g, sgu_ln_g, sgu_ln_b, w_spatial, b_spatial, w_gate, b_gate, w_br_attn, w_br_sgu, w_out, w_ff1, w_ff2, final_g):
    assert x.shape == (1, N_TOK, D) and ctx.shape == (1, CTX, D) and w_ada.shape[0] == 1
    bf = jnp.bfloat16
    x2d = x[0]
    ctx2d = ctx[0]

    ct = jnp.stack([c[0], c_ctx], axis=1)
    mod = _ada(ct, w_ada[0], b_ada[0][None, :])

    w_l = w_in[0]
    w_in_b = jnp.concatenate([_head_layout(w_l[:, :A_W]), _head_layout(w_l[:, A_W:2 * A_W]),
                              w_l[:, 2 * A_W:]], axis=1).astype(bf)
    cos_t, sin_t = _rope_tables(N_TOK)
    g1 = norm1_g[0][None, :]

    kc, vct = _ctx_proj(ctx2d, mod, g1, w_in_b)
    hx, q, k, vt, sx = _in_proj(x2d, mod, g1, w_in_b, cos_t, sin_t,
                                w_spatial[0].astype(bf), b_spatial[0].T,
                                sgu_ln_g[0][None, :], sgu_ln_b[0][None, :])
    ax = _attn(q, k, vt, kc, vct, lam_q1, lam_k1, lam_q2, lam_k2, subln_g[0][:, None])
    m = _merge(hx, ax, sx, w_gate[0].astype(bf), b_gate[0][None, :],
               w_br_attn[0].astype(bf), w_br_sgu[0].astype(bf))
    x1, h2 = _out_proj(m, w_out[0].astype(bf), x2d, mod, norm2_g[0][None, :])
    out = _ffn(h2, w_ff1[0].astype(bf), w_ff2[0].astype(bf), x1, mod, final_g[None, :])
    return out[None]
```

```python
import functools
import math

import jax
import jax.numpy as jnp
from jax import lax
from jax.experimental import pallas as pl
from jax.experimental.pallas import tpu as pltpu

D = 2048
N_TOK = 8192
GRID_W = 64
CTX = 256
N_HEADS = 8
HEAD_DIM = 64
PAIR = 2 * HEAD_DIM
A_W = N_HEADS * PAIR
SGU_W = D // 2
N_GROUPS = 8
GROUP_DIM = SGU_W // N_GROUPS
CHUNK = 128
FF = 4 * D
IN_W = 3 * A_W + 2 * SGU_W
ROPE_BASE = 10000.0
EPS = 1e-6
LAM_INIT = 0.8 - 0.6 * math.exp(-0.3 * 0)
Q_SCALE = HEAD_DIM ** -0.5 * math.log2(math.e)

V7X_VMEM_BYTES = 64 * 1024 * 1024
VMEM_LIMIT = V7X_VMEM_BYTES - 8 * 1024 * 1024
LANES = 128

TM_IN = 512
TQ = 256
TK = 768
N_SLOTS = 3
DOT_ROWS = 256
SOFTMAX_ROWS = 64
GROWTH_LIMIT = 64.0
TM_MERGE = 1024
TN_MERGE = 512
TM_OUT = 512
TM_FF = 512
TF = 1024
TN_ADA = 1024

NEG_BIG = -1e30


def _params(*sem, flags=None):
    return pltpu.CompilerParams(dimension_semantics=sem, vmem_limit_bytes=VMEM_LIMIT, flags=flags)


def _gelu_tanh(x):
    return 0.5 * x * (1.0 + jnp.tanh(math.sqrt(2.0 / math.pi) * (x + 0.044715 * (x * x * x))))


def _mod_norm(xf, g, shift, scale):
    y = xf * lax.rsqrt(jnp.mean(xf * xf, axis=-1, keepdims=True) + EPS)
    return (y * g) * (1.0 + scale) + shift


def _ada_kernel(ct_ref, w_ref, b_ref, o_ref):
    w = w_ref[...]
    rows = []
    for r in range(2):
        cv = ct_ref[:, r:r + 1]
        a = cv * jax.nn.sigmoid(cv)
        rows.append(jnp.sum(a * w, axis=0, keepdims=True))
    o_ref[...] = jnp.concatenate(rows, axis=0) + b_ref[...]


def _ada(ct, w, b):
    n_out = w.shape[1]
    return pl.pallas_call(
        _ada_kernel,
        grid=(n_out // TN_ADA,),
        in_specs=[
            pl.BlockSpec((D, 2), lambda j: (0, 0)),
            pl.BlockSpec((D, TN_ADA), lambda j: (0, j)),
            pl.BlockSpec((1, TN_ADA), lambda j: (0, j)),
        ],
        out_specs=pl.BlockSpec((2, TN_ADA), lambda j: (0, j)),
        out_shape=jax.ShapeDtypeStruct((2, n_out), jnp.float32),
        compiler_params=_params("parallel"),
    )(ct, w, b)


def _ctx_kernel(x_ref, mod_ref, g_ref, w_ref, k_ref, vt_ref, h_s):
    j = pl.program_id(0)

    @pl.when(j == 0)
    def _():
        h = _mod_norm(x_ref[...], g_ref[...], mod_ref[1:2, 0:D], mod_ref[1:2, D:2 * D])
        h_s[...] = h.astype(jnp.bfloat16)

    acc = jnp.dot(h_s[...], w_ref[...], preferred_element_type=jnp.float32)

    @pl.when(j == 0)
    def _():
        k_ref[...] = acc.astype(jnp.bfloat16)

    @pl.when(j == 1)
    def _():
        vt_ref[...] = acc.T.astype(jnp.bfloat16)


def _ctx_proj(ctx2d, mod, g1, w_in):
    return pl.pallas_call(
        _ctx_kernel,
        grid=(2,),
        in_specs=[
            pl.BlockSpec((CTX, D), lambda j: (0, 0)),
            pl.BlockSpec((2, 6 * D), lambda j: (0, 0)),
            pl.BlockSpec((1, D), lambda j: (0, 0)),
            pl.BlockSpec((D, A_W), lambda j: (0, j + 1)),
        ],
        out_specs=[
            pl.BlockSpec((CTX, A_W), lambda j: (0, 0)),
            pl.BlockSpec((A_W, CTX), lambda j: (0, 0)),
        ],
        out_shape=[
            jax.ShapeDtypeStruct((CTX, A_W), jnp.bfloat16),
            jax.ShapeDtypeStruct((A_W, CTX), jnp.bfloat16),
        ],
        scratch_shapes=[pltpu.VMEM((CTX, D), jnp.bfloat16)],
        compiler_params=_params("arbitrary"),
    )(ctx2d, mod, g1, w_in)


def _rope(t, c, s):
    outs = []
    for h in range(N_HEADS):
        blk = t[:, h * PAIR:(h + 1) * PAIR]
        outs.append(blk * c + pltpu.roll(blk, PAIR // 2, axis=1) * s)
    return jnp.concatenate(outs, axis=1)


def _in_kernel(x_ref, mod_ref, g_ref, w_ref, cos_ref, sin_ref, ws_ref, bst_ref, lng_ref, lnb_ref,
               hx_ref, q_ref, k_ref, vt_ref, sx_ref, u_s):
    j = pl.program_id(1)

    @pl.when(j == 0)
    def _():
        h = _mod_norm(x_ref[...], g_ref[...], mod_ref[0:1, 0:D], mod_ref[0:1, D:2 * D])
        hx_ref[...] = h.astype(jnp.bfloat16)

    acc = jnp.dot(hx_ref[...], w_ref[...], preferred_element_type=jnp.float32)

    @pl.when(j == 0)
    def _():
        q_ref[...] = (_rope(acc, cos_ref[...], sin_ref[...]) * Q_SCALE).astype(jnp.bfloat16)

    @pl.when(j == 1)
    def _():
        k_ref[...] = _rope(acc, cos_ref[...], sin_ref[...]).astype(jnp.bfloat16)

    @pl.when(j == 2)
    def _():
        vt_ref[...] = acc.T.astype(jnp.bfloat16)

    @pl.when(j == 3)
    def _():
        u_s[...] = _gelu_tanh(acc)

    @pl.when(j == 4)
    def _():
        v = _gelu_tanh(acc)
        mu = jnp.mean(v, axis=-1, keepdims=True)
        vc = v - mu
        var = jnp.mean(vc * vc, axis=-1, keepdims=True)
        vn = ((vc * lax.rsqrt(var + EPS)) * lng_ref[...] + lnb_ref[...]).astype(jnp.bfloat16)
        n_chunks = TM_IN // CHUNK
        for g in range(N_GROUPS):
            cols = slice(g * GROUP_DIM, (g + 1) * GROUP_DIM)
            rhs = jnp.concatenate([vn[c * CHUNK:(c + 1) * CHUNK, cols] for c in range(n_chunks)], axis=1)
            mixed = jnp.dot(ws_ref[g], rhs, preferred_element_type=jnp.float32) + bst_ref[:, g:g + 1]
            for c in range(n_chunks):
                rows = slice(c * CHUNK, (c + 1) * CHUNK)
                sx_ref[rows, cols] = (u_s[rows, cols] * mixed[:, c * GROUP_DIM:(c + 1) * GROUP_DIM]
                                      ).astype(jnp.bfloat16)


def _in_proj(x2d, mod, g1, w_in, cos_t, sin_t, ws, bst, lng, lnb):
    n = x2d.shape[0]
    row = lambda i, j: (i, 0)
    const = lambda i, j: (0, 0)
    return pl.pallas_call(
        _in_kernel,
        grid=(n // TM_IN, IN_W // A_W),
        in_specs=[
            pl.BlockSpec((TM_IN, D), row),
            pl.BlockSpec((2, 6 * D), const),
            pl.BlockSpec((1, D), const),
            pl.BlockSpec((D, A_W), lambda i, j: (0, j)),
            pl.BlockSpec((TM_IN, PAIR), row),
            pl.BlockSpec((TM_IN, PAIR), row),
            pl.BlockSpec((N_GROUPS, CHUNK, CHUNK), lambda i, j: (0, 0, 0)),
            pl.BlockSpec((CHUNK, N_GROUPS), const),
            pl.BlockSpec((1, SGU_W), const),
            pl.BlockSpec((1, SGU_W), const),
        ],
        out_specs=[
            pl.BlockSpec((TM_IN, D), row),
            pl.BlockSpec((TM_IN, A_W), row),
            pl.BlockSpec((TM_IN, A_W), row),
            pl.BlockSpec((A_W, TM_IN), lambda i, j: (0, i)),
            pl.BlockSpec((TM_IN, SGU_W), row),
        ],
        out_shape=[
            jax.ShapeDtypeStruct((n, D), jnp.bfloat16),
            jax.ShapeDtypeStruct((n, A_W), jnp.bfloat16),
            jax.ShapeDtypeStruct((n, A_W), jnp.bfloat16),
            jax.ShapeDtypeStruct((A_W, n), jnp.bfloat16),
            jax.ShapeDtypeStruct((n, SGU_W), jnp.bfloat16),
        ],
        scratch_shapes=[pltpu.VMEM((TM_IN, SGU_W), jnp.float32)],
        compiler_params=_params("parallel", "arbitrary"),
    )(x2d, mod, g1, w_in, cos_t, sin_t, ws, bst, lng, lnb)


def _attn_kernel(q_ref, k_ref, vt_ref, kc_ref, vct_ref, lq1_ref, lk1_ref, lq2_ref, lk2_ref, sg_ref,
                 o_ref, kk_s, vv_s, p_s, acc_s, l_s):
    n_lat = k_ref.shape[0]

    @pl.when(pl.program_id(1) == 0)
    def _():
        kk_s[0:n_lat, :] = k_ref[...]
        kk_s[n_lat:, :] = kc_ref[...]
        vv_s[:, 0:n_lat] = vt_ref[...]
        vv_s[:, n_lat:] = vct_ref[...]

    q = q_ref[...]
    lane = lax.broadcasted_iota(jnp.int32, q.shape, 1)
    first = (lane % HEAD_DIM) < (HEAD_DIM // 2)
    zero = jnp.zeros_like(q)
    qq = jnp.concatenate([jnp.where(first, q, zero), jnp.where(first, zero, q)], axis=0)

    n_blocks = kk_s.shape[0] // TK
    width = 2 * TQ
    stat = (1, width)

    def raw_scores(start, rows):
        return lax.dot_general(kk_s[pl.ds(start, rows), :], qq, (((1,), (1,)), ((), ())),
                               preferred_element_type=jnp.float32)

    def fold8(x, op):
        return op(x.reshape(x.shape[0] // 8, 8, width), axis=0)

    def values(t, p, alpha):
        start = pl.multiple_of(t * TK, TK)
        acc_s[...] = alpha * acc_s[...] + jnp.dot(vv_s[:, pl.ds(start, TK)], p,
                                                  preferred_element_type=jnp.float32)

    def probs(t, slot, stab):
        bm8 = jnp.full((8, width), NEG_BIG, jnp.float32)
        ls8 = jnp.zeros((8, width), jnp.float32)
        for c in range(TK // DOT_ROWS):
            s = raw_scores(pl.multiple_of(t * TK + c * DOT_ROWS, DOT_ROWS), DOT_ROWS)
            for r in range(DOT_ROWS // SOFTMAX_ROWS):
                sc = s[r * SOFTMAX_ROWS:(r + 1) * SOFTMAX_ROWS]
                p = jnp.exp2(sc - stab)
                row0 = c * DOT_ROWS + r * SOFTMAX_ROWS
                p_s[slot, row0:row0 + SOFTMAX_ROWS, :] = p.astype(jnp.bfloat16)
                bm8 = jnp.maximum(bm8, fold8(sc, jnp.max))
                ls8 = ls8 + fold8(p, jnp.sum)
        return jnp.max(bm8, axis=0, keepdims=True), jnp.sum(ls8, axis=0, keepdims=True)

    acc_s[...] = jnp.zeros(acc_s.shape, jnp.float32)
    stab = jnp.full(stat, NEG_BIG, jnp.float32)
    for c in range(TK // DOT_ROWS):
        stab = jnp.maximum(stab, jnp.max(raw_scores(c * DOT_ROWS, DOT_ROWS), axis=0, keepdims=True))
    bm, l = probs(0, 0, stab)
    carry = (stab, bm, jnp.ones(stat, jnp.float32), l, jnp.zeros(stat, jnp.float32))

    def step(t, cur, carry):
        stab, bm, alpha, l, rise = carry
        stab_next = jnp.maximum(stab, bm)
        alpha_next = jnp.exp2(stab - stab_next)
        nxt = (cur + 1) % N_SLOTS
        bm_next, ls = probs(t + 1, nxt, stab_next)
        values(t, p_s[cur], alpha)
        return stab_next, bm_next, alpha_next, alpha_next * l + ls, jnp.maximum(rise, bm - stab)

    def body(u, carry):
        for j in range(N_SLOTS):
            carry = step(N_SLOTS * u + j, j, carry)
        return carry

    n_trips = (n_blocks - 1) // N_SLOTS
    carry = lax.fori_loop(0, n_trips, body, carry)
    for t in range(N_SLOTS * n_trips, n_blocks - 1):
        carry = step(t, t % N_SLOTS, carry)
    stab, bm, alpha, l, rise = carry
    values(n_blocks - 1, p_s[(n_blocks - 1) % N_SLOTS], alpha)
    l_s[...] = l
    rise = jnp.maximum(rise, bm - stab)

    @pl.when(jnp.max(rise) > GROWTH_LIMIT)
    def _():
        acc_s[...] = jnp.zeros(acc_s.shape, jnp.float32)

        def safe_block(t, ml):
            m, l = ml
            s = raw_scores(pl.multiple_of(t * TK, TK), TK)
            m_new = jnp.maximum(m, jnp.max(s, axis=0, keepdims=True))
            alpha = jnp.exp2(m - m_new)
            p = jnp.exp2(s - m_new)
            values(t, p.astype(jnp.bfloat16), alpha)
            return m_new, alpha * l + jnp.sum(p, axis=0, keepdims=True)

        _, l_safe = lax.fori_loop(0, n_blocks, safe_block,
                                  (jnp.full(stat, NEG_BIG, jnp.float32), jnp.zeros(stat, jnp.float32)))
        l_s[...] = l_safe

    l = l_s[...]
    lam = (jnp.exp(jnp.sum(lq1_ref[...] * lk1_ref[...], axis=-1, keepdims=True))
           - jnp.exp(jnp.sum(lq2_ref[...] * lk2_ref[...], axis=-1, keepdims=True)) + LAM_INIT)
    o_both = acc_s[...] * (1.0 / l)
    o = o_both[:, :TQ] - lam * o_both[:, TQ:]
    y = o * lax.rsqrt(jnp.mean(o * o, axis=0, keepdims=True) + EPS)
    y = (y * sg_ref[...]) * (1.0 - LAM_INIT)
    o_ref[...] = y.T.astype(jnp.bfloat16)


def _attn(q, k, vt, kc, vct, lq1, lk1, lq2, lk2, sg_col):
    n = q.shape[0]
    n_keys = n + CTX
    assert n_keys % TK == 0 and n_keys // TK >= 3
    vec = pl.BlockSpec((1, HEAD_DIM), lambda h, i: (0, 0))
    return pl.pallas_call(
        _attn_kernel,
        grid=(N_HEADS, n // TQ),
        in_specs=[
            pl.BlockSpec((TQ, PAIR), lambda h, i: (i, h)),
            pl.BlockSpec((n, PAIR), lambda h, i: (0, h)),
            pl.BlockSpec((PAIR, n), lambda h, i: (h, 0)),
            pl.BlockSpec((CTX, PAIR), lambda h, i: (0, h)),
            pl.BlockSpec((PAIR, CTX), lambda h, i: (h, 0)),
            vec, vec, vec, vec,
            pl.BlockSpec((PAIR, 1), lambda h, i: (0, 0)),
        ],
        out_specs=pl.BlockSpec((TQ, PAIR), lambda h, i: (i, h)),
        out_shape=jax.ShapeDtypeStruct((n, A_W), jnp.bfloat16),
        scratch_shapes=[
            pltpu.VMEM((n_keys, PAIR), jnp.bfloat16),
            pltpu.VMEM((PAIR, n_keys), jnp.bfloat16),
            pltpu.VMEM((N_SLOTS, TK, 2 * TQ), jnp.bfloat16),
            pltpu.VMEM((PAIR, 2 * TQ), jnp.float32),
            pltpu.VMEM((1, 2 * TQ), jnp.float32),
        ],
        compiler_params=_params("arbitrary", "arbitrary"),
    )(q, k, vt, kc, vct, lq1, lk1, lq2, lk2, sg_col)


def _merge_kernel(hx_ref, ax_ref, sx_ref, wga_ref, wgs_ref, bga_ref, bgs_ref, wba_ref, wbs_ref, m_ref):
    hx = hx_ref[...]
    ga = jax.nn.sigmoid(jnp.dot(hx, wga_ref[...], preferred_element_type=jnp.float32) + bga_ref[...])
    gs = jax.nn.sigmoid(jnp.dot(hx, wgs_ref[...], preferred_element_type=jnp.float32) + bgs_ref[...])
    ba = jnp.dot(ax_ref[...], wba_ref[...], preferred_element_type=jnp.float32)
    bs = jnp.dot(sx_ref[...], wbs_ref[...], preferred_element_type=jnp.float32)
    m_ref[...] = (ga * ba + gs * bs).astype(jnp.bfloat16)


def _merge(hx, ax, sx, w_gate, b_gate, w_ba, w_bs):
    n = hx.shape[0]
    nj = D // TN_MERGE
    row = lambda i, j: (i, 0)
    return pl.pallas_call(
        _merge_kernel,
        grid=(n // TM_MERGE, nj),
        in_specs=[
            pl.BlockSpec((TM_MERGE, D), row),
            pl.BlockSpec((TM_MERGE, A_W), row),
            pl.BlockSpec((TM_MERGE, SGU_W), row),
            pl.BlockSpec((D, TN_MERGE), lambda i, j: (0, j)),
            pl.BlockSpec((D, TN_MERGE), lambda i, j: (0, j + nj)),
            pl.BlockSpec((1, TN_MERGE), lambda i, j: (0, j)),
            pl.BlockSpec((1, TN_MERGE), lambda i, j: (0, j + nj)),
            pl.BlockSpec((A_W, TN_MERGE), lambda i, j: (0, j)),
            pl.BlockSpec((SGU_W, TN_MERGE), lambda i, j: (0, j)),
        ],
        out_specs=pl.BlockSpec((TM_MERGE, TN_MERGE), lambda i, j: (i, j)),
        out_shape=jax.ShapeDtypeStruct((n, D), jnp.bfloat16),
        compiler_params=_params("parallel", "arbitrary"),
    )(hx, ax, sx, w_gate, w_gate, b_gate, b_gate, w_ba, w_bs)


def _out_kernel(m_ref, w_ref, x_ref, mod_ref, g_ref, x1_ref, h2_ref):
    y = jnp.dot(m_ref[...], w_ref[...], preferred_element_type=jnp.float32)
    x1 = x_ref[...] + mod_ref[0:1, 2 * D:3 * D] * y
    x1_ref[...] = x1
    h2 = _mod_norm(x1, g_ref[...], mod_ref[0:1, 3 * D:4 * D], mod_ref[0:1, 4 * D:5 * D])
    h2_ref[...] = h2.astype(jnp.bfloat16)


def _out_proj(m, w_out, x2d, mod, g2):
    n = m.shape[0]
    row = lambda i: (i, 0)
    const = lambda i: (0, 0)
    return pl.pallas_call(
        _out_kernel,
        grid=(n // TM_OUT,),
        in_specs=[
            pl.BlockSpec((TM_OUT, D), row),
            pl.BlockSpec((D, D), const),
            pl.BlockSpec((TM_OUT, D), row),
            pl.BlockSpec((2, 6 * D), const),
            pl.BlockSpec((1, D), const),
        ],
        out_specs=[pl.BlockSpec((TM_OUT, D), row), pl.BlockSpec((TM_OUT, D), row)],
        out_shape=[jax.ShapeDtypeStruct((n, D), jnp.float32), jax.ShapeDtypeStruct((n, D), jnp.bfloat16)],
        compiler_params=_params("parallel"),
    )(m, w_out, x2d, mod, g2)


def _ffn_kernel(h_ref, w1_ref, w2_ref, x1_ref, mod_ref, fg_ref, o_ref, acc_s):
    f = pl.program_id(1)

    @pl.when(f == 0)
    def _():
        acc_s[...] = jnp.zeros(acc_s.shape, jnp.float32)

    a = jnp.maximum(jnp.dot(h_ref[...], w1_ref[...], preferred_element_type=jnp.float32), 0.0)
    acc_s[...] += jnp.dot((a * a).astype(jnp.bfloat16), w2_ref[...], preferred_element_type=jnp.float32)

    @pl.when(f == pl.num_programs(1) - 1)
    def _():
        x2 = x1_ref[...] + mod_ref[0:1, 5 * D:6 * D] * acc_s[...]
        y = x2 * lax.rsqrt(jnp.mean(x2 * x2, axis=-1, keepdims=True) + EPS)
        o_ref[...] = y * fg_ref[...]


def _ffn(h2, w1, w2, x1, mod, fg):
    n = h2.shape[0]
    row = lambda i, f: (i, 0)
    const = lambda i, f: (0, 0)
    return pl.pallas_call(
        _ffn_kernel,
        grid=(n // TM_FF, FF // TF),
        in_specs=[
            pl.BlockSpec((TM_FF, D), row),
            pl.BlockSpec((D, TF), lambda i, f: (0, f)),
            pl.BlockSpec((TF, D), lambda i, f: (f, 0)),
            pl.BlockSpec((TM_FF, D), row),
            pl.BlockSpec((2, 6 * D), const),
            pl.BlockSpec((1, D), const),
        ],
        out_specs=pl.BlockSpec((TM_FF, D), row),
        out_shape=jax.ShapeDtypeStruct((n, D), jnp.float32),
        scratch_shapes=[pltpu.VMEM((TM_FF, D), jnp.float32)],
        compiler_params=_params("parallel", "arbitrary"),
    )(h2, w1, w2, x1, mod, fg)


def _head_layout(w):
    w = w.reshape(D, N_HEADS, 2, HEAD_DIM // 2, 2)
    return w.transpose(0, 1, 4, 2, 3).reshape(D, A_W)


def _rope_tables(n_tokens):
    rows = n_tokens // GRID_W
    r, col = jnp.meshgrid(jnp.arange(rows, dtype=jnp.float32),
                          jnp.arange(GRID_W, dtype=jnp.float32), indexing="ij")
    n_freq = HEAD_DIM // 4
    inv = ROPE_BASE ** (-jnp.arange(n_freq, dtype=jnp.float32) / n_freq)
    ang = jnp.concatenate([r.reshape(-1, 1) * inv, col.reshape(-1, 1) * inv], axis=-1)
    cos, sin = jnp.cos(ang), jnp.sin(ang)
    return jnp.tile(cos, (1, 4)), jnp.concatenate([-sin, -sin, sin, sin], axis=-1)


def kernel(x, c, ctx, c_ctx, w_ada, b_ada, norm1_g, norm2_g, w_in, lam_q1, lam_k1, lam_q2, lam_k2, subln_g, sgu_ln_g, sgu_ln_b, w_spatial, b_spatial, w_gate, b_gate, w_br_attn, w_br_sgu, w_out, w_ff1, w_ff2, final_g):
    assert x.shape == (1, N_TOK, D) and ctx.shape == (1, CTX, D) and w_ada.shape[0] == 1
    bf = jnp.bfloat16
    x2d = x[0]
    ctx2d = ctx[0]

    ct = jnp.stack([c[0], c_ctx], axis=1)
    mod = _ada(ct, w_ada[0], b_ada[0][None, :])

    w_l = w_in[0]
    w_in_b = jnp.concatenate([_head_layout(w_l[:, :A_W]), _head_layout(w_l[:, A_W:2 * A_W]),
                              w_l[:, 2 * A_W:]], axis=1).astype(bf)
    cos_t, sin_t = _rope_tables(N_TOK)
    g1 = norm1_g[0][None, :]

    kc, vct = _ctx_proj(ctx2d, mod, g1, w_in_b)
    hx, q, k, vt, sx = _in_proj(x2d, mod, g1, w_in_b, cos_t, sin_t,
                                w_spatial[0].astype(bf), b_spatial[0].T,
                                sgu_ln_g[0][None, :], sgu_ln_b[0][None, :])
    ax = _attn(q, k, vt, kc, vct, lam_q1, lam_k1, lam_q2, lam_k2, subln_g[0][:, None])
    m = _merge(hx, ax, sx, w_gate[0].astype(bf), b_gate[0][None, :],
               w_br_attn[0].astype(bf), w_br_sgu[0].astype(bf))
    x1, h2 = _out_proj(m, w_out[0].astype(bf), x2d, mod, norm2_g[0][None, :])
    out = _ffn(h2, w_ff1[0].astype(bf), w_ff2[0].astype(bf), x1, mod, final_g[None, :])
    return out[None]
```

```python
import functools
import math

import jax
import jax.numpy as jnp
from jax import lax
from jax.experimental import pallas as pl
from jax.experimental.pallas import tpu as pltpu

D = 2048
N_TOK = 8192
GRID_W = 64
CTX = 256
N_HEADS = 8
HEAD_DIM = 64
PAIR = 2 * HEAD_DIM
A_W = N_HEADS * PAIR
SGU_W = D // 2
N_GROUPS = 8
GROUP_DIM = SGU_W // N_GROUPS
CHUNK = 128
FF = 4 * D
IN_W = 3 * A_W + 2 * SGU_W
ROPE_BASE = 10000.0
EPS = 1e-6
LAM_INIT = 0.8 - 0.6 * math.exp(-0.3 * 0)
Q_SCALE = HEAD_DIM ** -0.5 * math.log2(math.e)

V7X_VMEM_BYTES = 64 * 1024 * 1024
VMEM_LIMIT = V7X_VMEM_BYTES - 8 * 1024 * 1024
LANES = 128

TM_IN = 512
TQ = 512
TK = 768
N_SLOTS = 3
DOT_ROWS = 256
SOFTMAX_ROWS = 32
GROWTH_LIMIT = 64.0
TM_MERGE = 1024
TN_MERGE = 512
TM_OUT = 512
TM_FF = 512
TF = 1024
TN_ADA = 1024

NEG_BIG = -1e30


def _params(*sem, flags=None):
    return pltpu.CompilerParams(dimension_semantics=sem, vmem_limit_bytes=VMEM_LIMIT, flags=flags)


def _gelu_tanh(x):
    return 0.5 * x * (1.0 + jnp.tanh(math.sqrt(2.0 / math.pi) * (x + 0.044715 * (x * x * x))))


def _mod_norm(xf, g, shift, scale):
    y = xf * lax.rsqrt(jnp.mean(xf * xf, axis=-1, keepdims=True) + EPS)
    return (y * g) * (1.0 + scale) + shift


def _ada_kernel(ct_ref, w_ref, b_ref, o_ref):
    w = w_ref[...]
    rows = []
    for r in range(2):
        cv = ct_ref[:, r:r + 1]
        a = cv * jax.nn.sigmoid(cv)
        rows.append(jnp.sum(a * w, axis=0, keepdims=True))
    o_ref[...] = jnp.concatenate(rows, axis=0) + b_ref[...]


def _ada(ct, w, b):
    n_out = w.shape[1]
    return pl.pallas_call(
        _ada_kernel,
        grid=(n_out // TN_ADA,),
        in_specs=[
            pl.BlockSpec((D, 2), lambda j: (0, 0)),
            pl.BlockSpec((D, TN_ADA), lambda j: (0, j)),
            pl.BlockSpec((1, TN_ADA), lambda j: (0, j)),
        ],
        out_specs=pl.BlockSpec((2, TN_ADA), lambda j: (0, j)),
        out_shape=jax.ShapeDtypeStruct((2, n_out), jnp.float32),
        compiler_params=_params("parallel"),
    )(ct, w, b)


def _ctx_kernel(x_ref, mod_ref, g_ref, w_ref, k_ref, vt_ref, h_s):
    j = pl.program_id(0)

    @pl.when(j == 0)
    def _():
        h = _mod_norm(x_ref[...], g_ref[...], mod_ref[1:2, 0:D], mod_ref[1:2, D:2 * D])
        h_s[...] = h.astype(jnp.bfloat16)

    acc = jnp.dot(h_s[...], w_ref[...], preferred_element_type=jnp.float32)

    @pl.when(j == 0)
    def _():
        k_ref[...] = acc.astype(jnp.bfloat16)

    @pl.when(j == 1)
    def _():
        vt_ref[...] = acc.T.astype(jnp.bfloat16)


def _ctx_proj(ctx2d, mod, g1, w_in):
    return pl.pallas_call(
        _ctx_kernel,
        grid=(2,),
        in_specs=[
            pl.BlockSpec((CTX, D), lambda j: (0, 0)),
            pl.BlockSpec((2, 6 * D), lambda j: (0, 0)),
            pl.BlockSpec((1, D), lambda j: (0, 0)),
            pl.BlockSpec((D, A_W), lambda j: (0, j + 1)),
        ],
        out_specs=[
            pl.BlockSpec((CTX, A_W), lambda j: (0, 0)),
            pl.BlockSpec((A_W, CTX), lambda j: (0, 0)),
        ],
        out_shape=[
            jax.ShapeDtypeStruct((CTX, A_W), jnp.bfloat16),
            jax.ShapeDtypeStruct((A_W, CTX), jnp.bfloat16),
        ],
        scratch_shapes=[pltpu.VMEM((CTX, D), jnp.bfloat16)],
        compiler_params=_params("arbitrary"),
    )(ctx2d, mod, g1, w_in)


def _rope(t, c, s):
    outs = []
    for h in range(N_HEADS):
        blk = t[:, h * PAIR:(h + 1) * PAIR]
        outs.append(blk * c + pltpu.roll(blk, PAIR // 2, axis=1) * s)
    return jnp.concatenate(outs, axis=1)


def _in_kernel(x_ref, mod_ref, g_ref, w_ref, cos_ref, sin_ref, ws_ref, bst_ref, lng_ref, lnb_ref,
               hx_ref, q_ref, k_ref, vt_ref, sx_ref, u_s):
    j = pl.program_id(1)

    @pl.when(j == 0)
    def _():
        h = _mod_norm(x_ref[...], g_ref[...], mod_ref[0:1, 0:D], mod_ref[0:1, D:2 * D])
        hx_ref[...] = h.astype(jnp.bfloat16)

    acc = jnp.dot(hx_ref[...], w_ref[...], preferred_element_type=jnp.float32)

    @pl.when(j == 0)
    def _():
        q_ref[...] = (_rope(acc, cos_ref[...], sin_ref[...]) * Q_SCALE).astype(jnp.bfloat16)

    @pl.when(j == 1)
    def _():
        k_ref[...] = _rope(acc, cos_ref[...], sin_ref[...]).astype(jnp.bfloat16)

    @pl.when(j == 2)
    def _():
        vt_ref[...] = acc.T.astype(jnp.bfloat16)

    @pl.when(j == 3)
    def _():
        u_s[...] = _gelu_tanh(acc)

    @pl.when(j == 4)
    def _():
        v = _gelu_tanh(acc)
        mu = jnp.mean(v, axis=-1, keepdims=True)
        vc = v - mu
        var = jnp.mean(vc * vc, axis=-1, keepdims=True)
        vn = ((vc * lax.rsqrt(var + EPS)) * lng_ref[...] + lnb_ref[...]).astype(jnp.bfloat16)
        n_chunks = TM_IN // CHUNK
        for g in range(N_GROUPS):
            cols = slice(g * GROUP_DIM, (g + 1) * GROUP_DIM)
            rhs = jnp.concatenate([vn[c * CHUNK:(c + 1) * CHUNK, cols] for c in range(n_chunks)], axis=1)
            mixed = jnp.dot(ws_ref[g], rhs, preferred_element_type=jnp.float32) + bst_ref[:, g:g + 1]
            for c in range(n_chunks):
                rows = slice(c * CHUNK, (c + 1) * CHUNK)
                sx_ref[rows, cols] = (u_s[rows, cols] * mixed[:, c * GROUP_DIM:(c + 1) * GROUP_DIM]
                                      ).astype(jnp.bfloat16)


def _in_proj(x2d, mod, g1, w_in, cos_t, sin_t, ws, bst, lng, lnb):
    n = x2d.shape[0]
    row = lambda i, j: (i, 0)
    const = lambda i, j: (0, 0)
    return pl.pallas_call(
        _in_kernel,
        grid=(n // TM_IN, IN_W // A_W),
        in_specs=[
            pl.BlockSpec((TM_IN, D), row),
            pl.BlockSpec((2, 6 * D), const),
            pl.BlockSpec((1, D), const),
            pl.BlockSpec((D, A_W), lambda i, j: (0, j)),
            pl.BlockSpec((TM_IN, PAIR), row),
            pl.BlockSpec((TM_IN, PAIR), row),
            pl.BlockSpec((N_GROUPS, CHUNK, CHUNK), lambda i, j: (0, 0, 0)),
            pl.BlockSpec((CHUNK, N_GROUPS), const),
            pl.BlockSpec((1, SGU_W), const),
            pl.BlockSpec((1, SGU_W), const),
        ],
        out_specs=[
            pl.BlockSpec((TM_IN, D), row),
            pl.BlockSpec((TM_IN, A_W), row),
            pl.BlockSpec((TM_IN, A_W), row),
            pl.BlockSpec((A_W, TM_IN), lambda i, j: (0, i)),
            pl.BlockSpec((TM_IN, SGU_W), row),
        ],
        out_shape=[
            jax.ShapeDtypeStruct((n, D), jnp.bfloat16),
            jax.ShapeDtypeStruct((n, A_W), jnp.bfloat16),
            jax.ShapeDtypeStruct((n, A_W), jnp.bfloat16),
            jax.ShapeDtypeStruct((A_W, n), jnp.bfloat16),
            jax.ShapeDtypeStruct((n, SGU_W), jnp.bfloat16),
        ],
        scratch_shapes=[pltpu.VMEM((TM_IN, SGU_W), jnp.float32)],
        compiler_params=_params("parallel", "arbitrary"),
    )(x2d, mod, g1, w_in, cos_t, sin_t, ws, bst, lng, lnb)


def _attn_kernel(q_ref, k_ref, vt_ref, kc_ref, vct_ref, lq1_ref, lk1_ref, lq2_ref, lk2_ref, sg_ref,
                 o_ref, kk_s, vv_s, p_s, acc_s, l_s):
    n_lat = k_ref.shape[0]

    @pl.when(pl.program_id(1) == 0)
    def _():
        kk_s[0:n_lat, :] = k_ref[...]
        kk_s[n_lat:, :] = kc_ref[...]
        vv_s[:, 0:n_lat] = vt_ref[...]
        vv_s[:, n_lat:] = vct_ref[...]

    q = q_ref[...]
    lane = lax.broadcasted_iota(jnp.int32, q.shape, 1)
    first = (lane % HEAD_DIM) < (HEAD_DIM // 2)
    zero = jnp.zeros_like(q)
    qq = jnp.concatenate([jnp.where(first, q, zero), jnp.where(first, zero, q)], axis=0)

    n_blocks = kk_s.shape[0] // TK
    width = 2 * TQ
    stat = (1, width)

    def raw_scores(start, rows):
        return lax.dot_general(kk_s[pl.ds(start, rows), :], qq, (((1,), (1,)), ((), ())),
                               preferred_element_type=jnp.float32)

    def fold8(x, op):
        return op(x.reshape(x.shape[0] // 8, 8, width), axis=0)

    def values(t, p, alpha):
        start = pl.multiple_of(t * TK, TK)
        acc_s[...] = alpha * acc_s[...] + jnp.dot(vv_s[:, pl.ds(start, TK)], p,
                                                  preferred_element_type=jnp.float32)

    def probs(t, slot, stab):
        bm8 = jnp.full((8, width), NEG_BIG, jnp.float32)
        ls8 = jnp.zeros((8, width), jnp.float32)
        for c in range(TK // DOT_ROWS):
            s = raw_scores(pl.multiple_of(t * TK + c * DOT_ROWS, DOT_ROWS), DOT_ROWS)
            for r in range(DOT_ROWS // SOFTMAX_ROWS):
                sc = s[r * SOFTMAX_ROWS:(r + 1) * SOFTMAX_ROWS]
                p = jnp.exp2(sc - stab)
                row0 = c * DOT_ROWS + r * SOFTMAX_ROWS
                p_s[slot, row0:row0 + SOFTMAX_ROWS, :] = p.astype(jnp.bfloat16)
                bm8 = jnp.maximum(bm8, fold8(sc, jnp.max))
                ls8 = ls8 + fold8(p, jnp.sum)
        return jnp.max(bm8, axis=0, keepdims=True), jnp.sum(ls8, axis=0, keepdims=True)

    acc_s[...] = jnp.zeros(acc_s.shape, jnp.float32)
    stab = jnp.max(raw_scores(0, DOT_ROWS), axis=0, keepdims=True)
    bm, l = probs(0, 0, stab)
    carry = (stab, bm, jnp.ones(stat, jnp.float32), l, jnp.zeros(stat, jnp.float32))

    def step(t, cur, carry):
        stab, bm, alpha, l, rise = carry
        stab_next = jnp.maximum(stab, bm)
        alpha_next = jnp.exp2(stab - stab_next)
        nxt = (cur + 1) % N_SLOTS
        bm_next, ls = probs(t + 1, nxt, stab_next)
        values(t, p_s[cur], alpha)
        return stab_next, bm_next, alpha_next, alpha_next * l + ls, jnp.maximum(rise, bm - stab)

    def body(u, carry):
        for j in range(N_SLOTS):
            carry = step(N_SLOTS * u + j, j, carry)
        return carry

    n_trips = (n_blocks - 1) // N_SLOTS
    carry = lax.fori_loop(0, n_trips, body, carry)
    for t in range(N_SLOTS * n_trips, n_blocks - 1):
        carry = step(t, t % N_SLOTS, carry)
    stab, bm, alpha, l, rise = carry
    values(n_blocks - 1, p_s[(n_blocks - 1) % N_SLOTS], alpha)
    l_s[...] = l
    rise = jnp.maximum(rise, bm - stab)

    @pl.when(jnp.max(rise) > GROWTH_LIMIT)
    def _():
        acc_s[...] = jnp.zeros(acc_s.shape, jnp.float32)

        def safe_block(t, ml):
            m, l = ml
            s = raw_scores(pl.multiple_of(t * TK, TK), TK)
            m_new = jnp.maximum(m, jnp.max(s, axis=0, keepdims=True))
            alpha = jnp.exp2(m - m_new)
            p = jnp.exp2(s - m_new)
            values(t, p.astype(jnp.bfloat16), alpha)
            return m_new, alpha * l + jnp.sum(p, axis=0, keepdims=True)

        _, l_safe = lax.fori_loop(0, n_blocks, safe_block,
                                  (jnp.full(stat, NEG_BIG, jnp.float32), jnp.zeros(stat, jnp.float32)))
        l_s[...] = l_safe

    l = l_s[...]
    lam = (jnp.exp(jnp.sum(lq1_ref[...] * lk1_ref[...], axis=-1, keepdims=True))
           - jnp.exp(jnp.sum(lq2_ref[...] * lk2_ref[...], axis=-1, keepdims=True)) + LAM_INIT)
    o_both = acc_s[...] * (1.0 / l)
    o = o_both[:, :TQ] - lam * o_both[:, TQ:]
    y = o * lax.rsqrt(jnp.mean(o * o, axis=0, keepdims=True) + EPS)
    y = (y * sg_ref[...]) * (1.0 - LAM_INIT)
    o_ref[...] = y.T.astype(jnp.bfloat16)


def _attn(q, k, vt, kc, vct, lq1, lk1, lq2, lk2, sg_col):
    n = q.shape[0]
    n_keys = n + CTX
    assert n_keys % TK == 0 and n_keys // TK >= 3
    vec = pl.BlockSpec((1, HEAD_DIM), lambda h, i: (0, 0))
    return pl.pallas_call(
        _attn_kernel,
        grid=(N_HEADS, n // TQ),
        in_specs=[
            pl.BlockSpec((TQ, PAIR), lambda h, i: (i, h)),
            pl.BlockSpec((n, PAIR), lambda h, i: (0, h)),
            pl.BlockSpec((PAIR, n), lambda h, i: (h, 0)),
            pl.BlockSpec((CTX, PAIR), lambda h, i: (0, h)),
            pl.BlockSpec((PAIR, CTX), lambda h, i: (h, 0)),
            vec, vec, vec, vec,
            pl.BlockSpec((PAIR, 1), lambda h, i: (0, 0)),
        ],
        out_specs=pl.BlockSpec((TQ, PAIR), lambda h, i: (i, h)),
        out_shape=jax.ShapeDtypeStruct((n, A_W), jnp.bfloat16),
        scratch_shapes=[
            pltpu.VMEM((n_keys, PAIR), jnp.bfloat16),
            pltpu.VMEM((PAIR, n_keys), jnp.bfloat16),
            pltpu.VMEM((N_SLOTS, TK, 2 * TQ), jnp.bfloat16),
            pltpu.VMEM((PAIR, 2 * TQ), jnp.float32),
            pltpu.VMEM((1, 2 * TQ), jnp.float32),
        ],
        compiler_params=_params("arbitrary", "arbitrary"),
    )(q, k, vt, kc, vct, lq1, lk1, lq2, lk2, sg_col)


def _merge_kernel(hx_ref, ax_ref, sx_ref, wga_ref, wgs_ref, bga_ref, bgs_ref, wba_ref, wbs_ref, m_ref):
    hx = hx_ref[...]
    ga = jax.nn.sigmoid(jnp.dot(hx, wga_ref[...], preferred_element_type=jnp.float32) + bga_ref[...])
    gs = jax.nn.sigmoid(jnp.dot(hx, wgs_ref[...], preferred_element_type=jnp.float32) + bgs_ref[...])
    ba = jnp.dot(ax_ref[...], wba_ref[...], preferred_element_type=jnp.float32)
    bs = jnp.dot(sx_ref[...], wbs_ref[...], preferred_element_type=jnp.float32)
    m_ref[...] = (ga * ba + gs * bs).astype(jnp.bfloat16)


def _merge(hx, ax, sx, w_gate, b_gate, w_ba, w_bs):
    n = hx.shape[0]
    nj = D // TN_MERGE
    row = lambda i, j: (i, 0)
    return pl.pallas_call(
        _merge_kernel,
        grid=(n // TM_MERGE, nj),
        in_specs=[
            pl.BlockSpec((TM_MERGE, D), row),
            pl.BlockSpec((TM_MERGE, A_W), row),
            pl.BlockSpec((TM_MERGE, SGU_W), row),
            pl.BlockSpec((D, TN_MERGE), lambda i, j: (0, j)),
            pl.BlockSpec((D, TN_MERGE), lambda i, j: (0, j + nj)),
            pl.BlockSpec((1, TN_MERGE), lambda i, j: (0, j)),
            pl.BlockSpec((1, TN_MERGE), lambda i, j: (0, j + nj)),
            pl.BlockSpec((A_W, TN_MERGE), lambda i, j: (0, j)),
            pl.BlockSpec((SGU_W, TN_MERGE), lambda i, j: (0, j)),
        ],
        out_specs=pl.BlockSpec((TM_MERGE, TN_MERGE), lambda i, j: (i, j)),
        out_shape=jax.ShapeDtypeStruct((n, D), jnp.bfloat16),
        compiler_params=_params("parallel", "arbitrary"),
    )(hx, ax, sx, w_gate, w_gate, b_gate, b_gate, w_ba, w_bs)


def _out_kernel(m_ref, w_ref, x_ref, mod_ref, g_ref, x1_ref, h2_ref):
    y = jnp.dot(m_ref[...], w_ref[...], preferred_element_type=jnp.float32)
    x1 = x_ref[...] + mod_ref[0:1, 2 * D:3 * D] * y
    x1_ref[...] = x1
    h2 = _mod_norm(x1, g_ref[...], mod_ref[0:1, 3 * D:4 * D], mod_ref[0:1, 4 * D:5 * D])
    h2_ref[...] = h2.astype(jnp.bfloat16)


def _out_proj(m, w_out, x2d, mod, g2):
    n = m.shape[0]
    row = lambda i: (i, 0)
    const = lambda i: (0, 0)
    return pl.pallas_call(
        _out_kernel,
        grid=(n // TM_OUT,),
        in_specs=[
            pl.BlockSpec((TM_OUT, D), row),
            pl.BlockSpec((D, D), const),
            pl.BlockSpec((TM_OUT, D), row),
            pl.BlockSpec((2, 6 * D), const),
            pl.BlockSpec((1, D), const),
        ],
        out_specs=[pl.BlockSpec((TM_OUT, D), row), pl.BlockSpec((TM_OUT, D), row)],
        out_shape=[jax.ShapeDtypeStruct((n, D), jnp.float32), jax.ShapeDtypeStruct((n, D), jnp.bfloat16)],
        compiler_params=_params("parallel"),
    )(m, w_out, x2d, mod, g2)


def _ffn_kernel(h_ref, w1_ref, w2_ref, x1_ref, mod_ref, fg_ref, o_ref, acc_s):
    f = pl.program_id(1)

    @pl.when(f == 0)
    def _():
        acc_s[...] = jnp.zeros(acc_s.shape, jnp.float32)

    a = jnp.maximum(jnp.dot(h_ref[...], w1_ref[...], preferred_element_type=jnp.float32), 0.0)
    acc_s[...] += jnp.dot((a * a).astype(jnp.bfloat16), w2_ref[...], preferred_element_type=jnp.float32)

    @pl.when(f == pl.num_programs(1) - 1)
    def _():
        x2 = x1_ref[...] + mod_ref[0:1, 5 * D:6 * D] * acc_s[...]
        y = x2 * lax.rsqrt(jnp.mean(x2 * x2, axis=-1, keepdims=True) + EPS)
        o_ref[...] = y * fg_ref[...]


def _ffn(h2, w1, w2, x1, mod, fg):
    n = h2.shape[0]
    row = lambda i, f: (i, 0)
    const = lambda i, f: (0, 0)
    return pl.pallas_call(
        _ffn_kernel,
        grid=(n // TM_FF, FF // TF),
        in_specs=[
            pl.BlockSpec((TM_FF, D), row),
            pl.BlockSpec((D, TF), lambda i, f: (0, f)),
            pl.BlockSpec((TF, D), lambda i, f: (f, 0)),
            pl.BlockSpec((TM_FF, D), row),
            pl.BlockSpec((2, 6 * D), const),
            pl.BlockSpec((1, D), const),
        ],
        out_specs=pl.BlockSpec((TM_FF, D), row),
        out_shape=jax.ShapeDtypeStruct((n, D), jnp.float32),
        scratch_shapes=[pltpu.VMEM((TM_FF, D), jnp.float32)],
        compiler_params=_params("parallel", "arbitrary"),
    )(h2, w1, w2, x1, mod, fg)


def _head_layout(w):
    w = w.reshape(D, N_HEADS, 2, HEAD_DIM // 2, 2)
    return w.transpose(0, 1, 4, 2, 3).reshape(D, A_W)


def _rope_tables(n_tokens):
    rows = n_tokens // GRID_W
    r, col = jnp.meshgrid(jnp.arange(rows, dtype=jnp.float32),
                          jnp.arange(GRID_W, dtype=jnp.float32), indexing="ij")
    n_freq = HEAD_DIM // 4
    inv = ROPE_BASE ** (-jnp.arange(n_freq, dtype=jnp.float32) / n_freq)
    ang = jnp.concatenate([r.reshape(-1, 1) * inv, col.reshape(-1, 1) * inv], axis=-1)
    cos, sin = jnp.cos(ang), jnp.sin(ang)
    return jnp.tile(cos, (1, 4)), jnp.concatenate([-sin, -sin, sin, sin], axis=-1)


def kernel(x, c, ctx, c_ctx, w_ada, b_ada, norm1_g, norm2_g, w_in, lam_q1, lam_k1, lam_q2, lam_k2, subln_g, sgu_ln_g, sgu_ln_b, w_spatial, b_spatial, w_gate, b_gate, w_br_attn, w_br_sgu, w_out, w_ff1, w_ff2, final_g):
    assert x.shape == (1, N_TOK, D) and ctx.shape == (1, CTX, D) and w_ada.shape[0] == 1
    bf = jnp.bfloat16
    x2d = x[0]
    ctx2d = ctx[0]

    ct = jnp.stack([c[0], c_ctx], axis=1)
    mod = _ada(ct, w_ada[0], b_ada[0][None, :])

    w_l = w_in[0]
    w_in_b = jnp.concatenate([_head_layout(w_l[:, :A_W]), _head_layout(w_l[:, A_W:2 * A_W]),
                              w_l[:, 2 * A_W:]], axis=1).astype(bf)
    cos_t, sin_t = _rope_tables(N_TOK)
    g1 = norm1_g[0][None, :]

    kc, vct = _ctx_proj(ctx2d, mod, g1, w_in_b)
    hx, q, k, vt, sx = _in_proj(x2d, mod, g1, w_in_b, cos_t, sin_t,
                                w_spatial[0].astype(bf), b_spatial[0].T,
                                sgu_ln_g[0][None, :], sgu_ln_b[0][None, :])
    ax = _attn(q, k, vt, kc, vct, lam_q1, lam_k1, lam_q2, lam_k2, subln_g[0][:, None])
    m = _merge(hx, ax, sx, w_gate[0].astype(bf), b_gate[0][None, :],
               w_br_attn[0].astype(bf), w_br_sgu[0].astype(bf))
    x1, h2 = _out_proj(m, w_out[0].astype(bf), x2d, mod, norm2_g[0][None, :])
    out = _ffn(h2, w_ff1[0].astype(bf), w_ff2[0].astype(bf), x1, mod, final_g[None, :])
    return out[None]
```

```python
import functools
import math

import jax
import jax.numpy as jnp
from jax import lax
from jax.experimental import pallas as pl
from jax.experimental.pallas import tpu as pltpu

D = 2048
N_TOK = 8192
GRID_W = 64
CTX = 256
N_HEADS = 8
HEAD_DIM = 64
PAIR = 2 * HEAD_DIM
A_W = N_HEADS * PAIR
SGU_W = D // 2
N_GROUPS = 8
GROUP_DIM = SGU_W // N_GROUPS
CHUNK = 128
FF = 4 * D
IN_W = 3 * A_W + 2 * SGU_W
ROPE_BASE = 10000.0
EPS = 1e-6
LAM_INIT = 0.8 - 0.6 * math.exp(-0.3 * 0)
Q_SCALE = HEAD_DIM ** -0.5 * math.log2(math.e)

V7X_VMEM_BYTES = 64 * 1024 * 1024
VMEM_LIMIT = V7X_VMEM_BYTES - 8 * 1024 * 1024
LANES = 128

TM_IN = 512
TQ = 256
TK = 768
N_SLOTS = 3
DOT_ROWS = 256
SOFTMAX_ROWS = 64
GROWTH_LIMIT = 64.0
TM_MERGE = 1024
TN_MERGE = 512
TM_OUT = 512
TM_FF = 512
TF = 1024
TN_ADA = 1024

NEG_BIG = -1e30


def _params(*sem, flags=None):
    return pltpu.CompilerParams(dimension_semantics=sem, vmem_limit_bytes=VMEM_LIMIT, flags=flags)


def _gelu_tanh(x):
    return 0.5 * x * (1.0 + jnp.tanh(math.sqrt(2.0 / math.pi) * (x + 0.044715 * (x * x * x))))


def _mod_norm(xf, g, shift, scale):
    y = xf * lax.rsqrt(jnp.mean(xf * xf, axis=-1, keepdims=True) + EPS)
    return (y * g) * (1.0 + scale) + shift


def _ada_kernel(ct_ref, w_ref, b_ref, o_ref):
    w = w_ref[...]
    rows = []
    for r in range(2):
        cv = ct_ref[:, r:r + 1]
        a = cv * jax.nn.sigmoid(cv)
        rows.append(jnp.sum(a * w, axis=0, keepdims=True))
    o_ref[...] = jnp.concatenate(rows, axis=0) + b_ref[...]


def _ada(ct, w, b):
    n_out = w.shape[1]
    return pl.pallas_call(
        _ada_kernel,
        grid=(n_out // TN_ADA,),
        in_specs=[
            pl.BlockSpec((D, 2), lambda j: (0, 0)),
            pl.BlockSpec((D, TN_ADA), lambda j: (0, j)),
            pl.BlockSpec((1, TN_ADA), lambda j: (0, j)),
        ],
        out_specs=pl.BlockSpec((2, TN_ADA), lambda j: (0, j)),
        out_shape=jax.ShapeDtypeStruct((2, n_out), jnp.float32),
        compiler_params=_params("parallel"),
    )(ct, w, b)


def _ctx_kernel(x_ref, mod_ref, g_ref, w_ref, k_ref, vt_ref, h_s):
    j = pl.program_id(0)

    @pl.when(j == 0)
    def _():
        h = _mod_norm(x_ref[...], g_ref[...], mod_ref[1:2, 0:D], mod_ref[1:2, D:2 * D])
        h_s[...] = h.astype(jnp.bfloat16)

    acc = jnp.dot(h_s[...], w_ref[...], preferred_element_type=jnp.float32)

    @pl.when(j == 0)
    def _():
        k_ref[...] = acc.astype(jnp.bfloat16)

    @pl.when(j == 1)
    def _():
        vt_ref[...] = acc.T.astype(jnp.bfloat16)


def _ctx_proj(ctx2d, mod, g1, w_in):
    return pl.pallas_call(
        _ctx_kernel,
        grid=(2,),
        in_specs=[
            pl.BlockSpec((CTX, D), lambda j: (0, 0)),
            pl.BlockSpec((2, 6 * D), lambda j: (0, 0)),
            pl.BlockSpec((1, D), lambda j: (0, 0)),
            pl.BlockSpec((D, A_W), lambda j: (0, j + 1)),
        ],
        out_specs=[
            pl.BlockSpec((CTX, A_W), lambda j: (0, 0)),
            pl.BlockSpec((A_W, CTX), lambda j: (0, 0)),
        ],
        out_shape=[
            jax.ShapeDtypeStruct((CTX, A_W), jnp.bfloat16),
            jax.ShapeDtypeStruct((A_W, CTX), jnp.bfloat16),
        ],
        scratch_shapes=[pltpu.VMEM((CTX, D), jnp.bfloat16)],
        compiler_params=_params("arbitrary"),
    )(ctx2d, mod, g1, w_in)


def _rope(t, c, s):
    outs = []
    for h in range(N_HEADS):
        blk = t[:, h * PAIR:(h + 1) * PAIR]
        outs.append(blk * c + pltpu.roll(blk, PAIR // 2, axis=1) * s)
    return jnp.concatenate(outs, axis=1)


def _in_kernel(x_ref, mod_ref, g_ref, w_ref, cos_ref, sin_ref, ws_ref, bst_ref, lng_ref, lnb_ref,
               hx_ref, q_ref, k_ref, vt_ref, sx_ref, u_s):
    j = pl.program_id(1)

    @pl.when(j == 0)
    def _():
        h = _mod_norm(x_ref[...], g_ref[...], mod_ref[0:1, 0:D], mod_ref[0:1, D:2 * D])
        hx_ref[...] = h.astype(jnp.bfloat16)

    acc = jnp.dot(hx_ref[...], w_ref[...], preferred_element_type=jnp.float32)

    @pl.when(j == 0)
    def _():
        q_ref[...] = (_rope(acc, cos_ref[...], sin_ref[...]) * Q_SCALE).astype(jnp.bfloat16)

    @pl.when(j == 1)
    def _():
        k_ref[...] = _rope(acc, cos_ref[...], sin_ref[...]).astype(jnp.bfloat16)

    @pl.when(j == 2)
    def _():
        vt_ref[...] = acc.T.astype(jnp.bfloat16)

    @pl.when(j == 3)
    def _():
        u_s[...] = _gelu_tanh(acc)

    @pl.when(j == 4)
    def _():
        v = _gelu_tanh(acc)
        mu = jnp.mean(v, axis=-1, keepdims=True)
        vc = v - mu
        var = jnp.mean(vc * vc, axis=-1, keepdims=True)
        vn = ((vc * lax.rsqrt(var + EPS)) * lng_ref[...] + lnb_ref[...]).astype(jnp.bfloat16)
        n_chunks = TM_IN // CHUNK
        for g in range(N_GROUPS):
            cols = slice(g * GROUP_DIM, (g + 1) * GROUP_DIM)
            rhs = jnp.concatenate([vn[c * CHUNK:(c + 1) * CHUNK, cols] for c in range(n_chunks)], axis=1)
            mixed = jnp.dot(ws_ref[g], rhs, preferred_element_type=jnp.float32) + bst_ref[:, g:g + 1]
            for c in range(n_chunks):
                rows = slice(c * CHUNK, (c + 1) * CHUNK)
                sx_ref[rows, cols] = (u_s[rows, cols] * mixed[:, c * GROUP_DIM:(c + 1) * GROUP_DIM]
                                      ).astype(jnp.bfloat16)


def _in_proj(x2d, mod, g1, w_in, cos_t, sin_t, ws, bst, lng, lnb):
    n = x2d.shape[0]
    row = lambda i, j: (i, 0)
    const = lambda i, j: (0, 0)
    return pl.pallas_call(
        _in_kernel,
        grid=(n // TM_IN, IN_W // A_W),
        in_specs=[
            pl.BlockSpec((TM_IN, D), row),
            pl.BlockSpec((2, 6 * D), const),
            pl.BlockSpec((1, D), const),
            pl.BlockSpec((D, A_W), lambda i, j: (0, j)),
            pl.BlockSpec((TM_IN, PAIR), row),
            pl.BlockSpec((TM_IN, PAIR), row),
            pl.BlockSpec((N_GROUPS, CHUNK, CHUNK), lambda i, j: (0, 0, 0)),
            pl.BlockSpec((CHUNK, N_GROUPS), const),
            pl.BlockSpec((1, SGU_W), const),
            pl.BlockSpec((1, SGU_W), const),
        ],
        out_specs=[
            pl.BlockSpec((TM_IN, D), row),
            pl.BlockSpec((TM_IN, A_W), row),
            pl.BlockSpec((TM_IN, A_W), row),
            pl.BlockSpec((A_W, TM_IN), lambda i, j: (0, i)),
            pl.BlockSpec((TM_IN, SGU_W), row),
        ],
        out_shape=[
            jax.ShapeDtypeStruct((n, D), jnp.bfloat16),
            jax.ShapeDtypeStruct((n, A_W), jnp.bfloat16),
            jax.ShapeDtypeStruct((n, A_W), jnp.bfloat16),
            jax.ShapeDtypeStruct((A_W, n), jnp.bfloat16),
            jax.ShapeDtypeStruct((n, SGU_W), jnp.bfloat16),
        ],
        scratch_shapes=[pltpu.VMEM((TM_IN, SGU_W), jnp.float32)],
        compiler_params=_params("parallel", "arbitrary"),
    )(x2d, mod, g1, w_in, cos_t, sin_t, ws, bst, lng, lnb)


def _attn_kernel(q_ref, k_ref, vt_ref, kc_ref, vct_ref, lq1_ref, lk1_ref, lq2_ref, lk2_ref, sg_ref,
                 o_ref, kk_s, vv_s, p_s, acc_s, l_s):
    n_lat = k_ref.shape[0]

    @pl.when(pl.program_id(1) == 0)
    def _():
        kk_s[0:n_lat, :] = k_ref[...]
        kk_s[n_lat:, :] = kc_ref[...]
        vv_s[:, 0:n_lat] = vt_ref[...]
        vv_s[:, n_lat:] = vct_ref[...]

    q = q_ref[...]
    lane = lax.broadcasted_iota(jnp.int32, q.shape, 1)
    first = (lane % HEAD_DIM) < (HEAD_DIM // 2)
    zero = jnp.zeros_like(q)
    qq = jnp.concatenate([jnp.where(first, q, zero), jnp.where(first, zero, q)], axis=0)

    n_blocks = kk_s.shape[0] // TK
    width = 2 * TQ
    stat = (1, width)

    def raw_scores(start, rows):
        return lax.dot_general(kk_s[pl.ds(start, rows), :], qq, (((1,), (1,)), ((), ())),
                               preferred_element_type=jnp.float32)

    def fold8(x, op):
        return op(x.reshape(x.shape[0] // 8, 8, width), axis=0)

    def values(t, p, alpha):
        start = pl.multiple_of(t * TK, TK)
        acc_s[...] = alpha * acc_s[...] + jnp.dot(vv_s[:, pl.ds(start, TK)], p,
                                                  preferred_element_type=jnp.float32)

    def probs(t, slot, stab):
        bm8 = jnp.full((8, width), NEG_BIG, jnp.float32)
        ls8 = jnp.zeros((8, width), jnp.float32)
        for c in range(TK // DOT_ROWS):
            s = raw_scores(pl.multiple_of(t * TK + c * DOT_ROWS, DOT_ROWS), DOT_ROWS)
            for r in range(DOT_ROWS // SOFTMAX_ROWS):
                sc = s[r * SOFTMAX_ROWS:(r + 1) * SOFTMAX_ROWS]
                p = jnp.exp2(sc - stab)
                row0 = c * DOT_ROWS + r * SOFTMAX_ROWS
                p_s[slot, row0:row0 + SOFTMAX_ROWS, :] = p.astype(jnp.bfloat16)
                bm8 = jnp.maximum(bm8, fold8(sc, jnp.max))
                ls8 = ls8 + fold8(p, jnp.sum)
        return jnp.max(bm8, axis=0, keepdims=True), jnp.sum(ls8, axis=0, keepdims=True)

    acc_s[...] = jnp.zeros(acc_s.shape, jnp.float32)
    stab = jnp.max(raw_scores(0, DOT_ROWS), axis=0, keepdims=True)
    bm, l = probs(0, 0, stab)
    carry = (stab, bm, jnp.ones(stat, jnp.float32), l, jnp.zeros(stat, jnp.float32))

    def step(t, cur, carry):
        stab, bm, alpha, l, rise = carry
        stab_next = jnp.maximum(stab, bm)
        alpha_next = jnp.exp2(stab - stab_next)
        nxt = (cur + 1) % N_SLOTS
        bm_next, ls = probs(t + 1, nxt, stab_next)
        values(t, p_s[cur], alpha)
        return stab_next, bm_next, alpha_next, alpha_next * l + ls, jnp.maximum(rise, bm - stab)

    def body(u, carry):
        for j in range(N_SLOTS):
            carry = step(N_SLOTS * u + j, j, carry)
        return carry

    n_trips = (n_blocks - 1) // N_SLOTS
    carry = lax.fori_loop(0, n_trips, body, carry)
    for t in range(N_SLOTS * n_trips, n_blocks - 1):
        carry = step(t, t % N_SLOTS, carry)
    stab, bm, alpha, l, rise = carry
    values(n_blocks - 1, p_s[(n_blocks - 1) % N_SLOTS], alpha)
    l_s[...] = l
    rise = jnp.maximum(rise, bm - stab)

    @pl.when(jnp.max(rise) > GROWTH_LIMIT)
    def _():
        acc_s[...] = jnp.zeros(acc_s.shape, jnp.float32)

        def safe_block(t, ml):
            m, l = ml
            s = raw_scores(pl.multiple_of(t * TK, TK), TK)
            m_new = jnp.maximum(m, jnp.max(s, axis=0, keepdims=True))
            alpha = jnp.exp2(m - m_new)
            p = jnp.exp2(s - m_new)
            values(t, p.astype(jnp.bfloat16), alpha)
            return m_new, alpha * l + jnp.sum(p, axis=0, keepdims=True)

        _, l_safe = lax.fori_loop(0, n_blocks, safe_block,
                                  (jnp.full(stat, NEG_BIG, jnp.float32), jnp.zeros(stat, jnp.float32)))
        l_s[...] = l_safe

    l = l_s[...]
    lam = (jnp.exp(jnp.sum(lq1_ref[...] * lk1_ref[...], axis=-1, keepdims=True))
           - jnp.exp(jnp.sum(lq2_ref[...] * lk2_ref[...], axis=-1, keepdims=True)) + LAM_INIT)
    o_both = acc_s[...] * (1.0 / l)
    o = o_both[:, :TQ] - lam * o_both[:, TQ:]
    y = o * lax.rsqrt(jnp.mean(o * o, axis=0, keepdims=True) + EPS)
    y = (y * sg_ref[...]) * (1.0 - LAM_INIT)
    o_ref[...] = y.T.astype(jnp.bfloat16)


def _attn(q, k, vt, kc, vct, lq1, lk1, lq2, lk2, sg_col):
    n = q.shape[0]
    n_keys = n + CTX
    assert n_keys % TK == 0 and n_keys // TK >= 3
    vec = pl.BlockSpec((1, HEAD_DIM), lambda h, i: (0, 0))
    return pl.pallas_call(
        _attn_kernel,
        grid=(N_HEADS, n // TQ),
        in_specs=[
            pl.BlockSpec((TQ, PAIR), lambda h, i: (i, h)),
            pl.BlockSpec((n, PAIR), lambda h, i: (0, h)),
            pl.BlockSpec((PAIR, n), lambda h, i: (h, 0)),
            pl.BlockSpec((CTX, PAIR), lambda h, i: (0, h)),
            pl.BlockSpec((PAIR, CTX), lambda h, i: (h, 0)),
            vec, vec, vec, vec,
            pl.BlockSpec((PAIR, 1), lambda h, i: (0, 0)),
        ],
        out_specs=pl.BlockSpec((TQ, PAIR), lambda h, i: (i, h)),
        out_shape=jax.ShapeDtypeStruct((n, A_W), jnp.bfloat16),
        scratch_shapes=[
            pltpu.VMEM((n_keys, PAIR), jnp.bfloat16),
            pltpu.VMEM((PAIR, n_keys), jnp.bfloat16),
            pltpu.VMEM((N_SLOTS, TK, 2 * TQ), jnp.bfloat16),
            pltpu.VMEM((PAIR, 2 * TQ), jnp.float32),
            pltpu.VMEM((1, 2 * TQ), jnp.float32),
        ],
        compiler_params=_params("arbitrary", "arbitrary"),
    )(q, k, vt, kc, vct, lq1, lk1, lq2, lk2, sg_col)


def _merge_kernel(hx_ref, ax_ref, sx_ref, wga_ref, wgs_ref, bga_ref, bgs_ref, wba_ref, wbs_ref, m_ref):
    hx = hx_ref[...]
    ga = jax.nn.sigmoid(jnp.dot(hx, wga_ref[...], preferred_element_type=jnp.float32) + bga_ref[...])
    gs = jax.nn.sigmoid(jnp.dot(hx, wgs_ref[...], preferred_element_type=jnp.float32) + bgs_ref[...])
    ba = jnp.dot(ax_ref[...], wba_ref[...], preferred_element_type=jnp.float32)
    bs = jnp.dot(sx_ref[...], wbs_ref[...], preferred_element_type=jnp.float32)
    m_ref[...] = (ga * ba + gs * bs).astype(jnp.bfloat16)


def _merge(hx, ax, sx, w_gate, b_gate, w_ba, w_bs):
    n = hx.shape[0]
    nj = D // TN_MERGE
    row = lambda i, j: (i, 0)
    return pl.pallas_call(
        _merge_kernel,
        grid=(n // TM_MERGE, nj),
        in_specs=[
            pl.BlockSpec((TM_MERGE, D), row),
            pl.BlockSpec((TM_MERGE, A_W), row),
            pl.BlockSpec((TM_MERGE, SGU_W), row),
            pl.BlockSpec((D, TN_MERGE), lambda i, j: (0, j)),
            pl.BlockSpec((D, TN_MERGE), lambda i, j: (0, j + nj)),
            pl.BlockSpec((1, TN_MERGE), lambda i, j: (0, j)),
            pl.BlockSpec((1, TN_MERGE), lambda i, j: (0, j + nj)),
            pl.BlockSpec((A_W, TN_MERGE), lambda i, j: (0, j)),
            pl.BlockSpec((SGU_W, TN_MERGE), lambda i, j: (0, j)),
        ],
        out_specs=pl.BlockSpec((TM_MERGE, TN_MERGE), lambda i, j: (i, j)),
        out_shape=jax.ShapeDtypeStruct((n, D), jnp.bfloat16),
        compiler_params=_params("parallel", "arbitrary"),
    )(hx, ax, sx, w_gate, w_gate, b_gate, b_gate, w_ba, w_bs)


def _out_kernel(m_ref, w_ref, x_ref, mod_ref, g_ref, x1_ref, h2_ref):
    y = jnp.dot(m_ref[...], w_ref[...], preferred_element_type=jnp.float32)
    x1 = x_ref[...] + mod_ref[0:1, 2 * D:3 * D] * y
    x1_ref[...] = x1
    h2 = _mod_norm(x1, g_ref[...], mod_ref[0:1, 3 * D:4 * D], mod_ref[0:1, 4 * D:5 * D])
    h2_ref[...] = h2.astype(jnp.bfloat16)


def _out_proj(m, w_out, x2d, mod, g2):
    n = m.shape[0]
    row = lambda i: (i, 0)
    const = lambda i: (0, 0)
    return pl.pallas_call(
        _out_kernel,
        grid=(n // TM_OUT,),
        in_specs=[
            pl.BlockSpec((TM_OUT, D), row),
            pl.BlockSpec((D, D), const),
            pl.BlockSpec((TM_OUT, D), row),
            pl.BlockSpec((2, 6 * D), const),
            pl.BlockSpec((1, D), const),
        ],
        out_specs=[pl.BlockSpec((TM_OUT, D), row), pl.BlockSpec((TM_OUT, D), row)],
        out_shape=[jax.ShapeDtypeStruct((n, D), jnp.float32), jax.ShapeDtypeStruct((n, D), jnp.bfloat16)],
        compiler_params=_params("parallel"),
    )(m, w_out, x2d, mod, g2)


def _ffn_kernel(h_ref, w1_ref, w2_ref, x1_ref, mod_ref, fg_ref, o_ref, acc_s):
    f = pl.program_id(1)

    @pl.when(f == 0)
    def _():
        acc_s[...] = jnp.zeros(acc_s.shape, jnp.float32)

    a = jnp.maximum(jnp.dot(h_ref[...], w1_ref[...], preferred_element_type=jnp.float32), 0.0)
    acc_s[...] += jnp.dot((a * a).astype(jnp.bfloat16), w2_ref[...], preferred_element_type=jnp.float32)

    @pl.when(f == pl.num_programs(1) - 1)
    def _():
        x2 = x1_ref[...] + mod_ref[0:1, 5 * D:6 * D] * acc_s[...]
        y = x2 * lax.rsqrt(jnp.mean(x2 * x2, axis=-1, keepdims=True) + EPS)
        o_ref[...] = y * fg_ref[...]


def _ffn(h2, w1, w2, x1, mod, fg):
    n = h2.shape[0]
    row = lambda i, f: (i, 0)
    const = lambda i, f: (0, 0)
    return pl.pallas_call(
        _ffn_kernel,
        grid=(n // TM_FF, FF // TF),
        in_specs=[
            pl.BlockSpec((TM_FF, D), row),
            pl.BlockSpec((D, TF), lambda i, f: (0, f)),
            pl.BlockSpec((TF, D), lambda i, f: (f, 0)),
            pl.BlockSpec((TM_FF, D), row),
            pl.BlockSpec((2, 6 * D), const),
            pl.BlockSpec((1, D), const),
        ],
        out_specs=pl.BlockSpec((TM_FF, D), row),
        out_shape=jax.ShapeDtypeStruct((n, D), jnp.float32),
        scratch_shapes=[pltpu.VMEM((TM_FF, D), jnp.float32)],
        compiler_params=_params("parallel", "arbitrary"),
    )(h2, w1, w2, x1, mod, fg)


def _head_layout(w):
    w = w.reshape(D, N_HEADS, 2, HEAD_DIM // 2, 2)
    return w.transpose(0, 1, 4, 2, 3).reshape(D, A_W)


def _rope_tables(n_tokens):
    rows = n_tokens // GRID_W
    r, col = jnp.meshgrid(jnp.arange(rows, dtype=jnp.float32),
                          jnp.arange(GRID_W, dtype=jnp.float32), indexing="ij")
    n_freq = HEAD_DIM // 4
    inv = ROPE_BASE ** (-jnp.arange(n_freq, dtype=jnp.float32) / n_freq)
    ang = jnp.concatenate([r.reshape(-1, 1) * inv, col.reshape(-1, 1) * inv], axis=-1)
    cos, sin = jnp.cos(ang), jnp.sin(ang)
    return jnp.tile(cos, (1, 4)), jnp.concatenate([-sin, -sin, sin, sin], axis=-1)


def kernel(x, c, ctx, c_ctx, w_ada, b_ada, norm1_g, norm2_g, w_in, lam_q1, lam_k1, lam_q2, lam_k2, subln_g, sgu_ln_g, sgu_ln_b, w_spatial, b_spatial, w_gate, b_gate, w_br_attn, w_br_sgu, w_out, w_ff1, w_ff2, final_g):
    assert x.shape == (1, N_TOK, D) and ctx.shape == (1, CTX, D) and w_ada.shape[0] == 1
    bf = jnp.bfloat16
    x2d = x[0]
    ctx2d = ctx[0]

    ct = jnp.stack([c[0], c_ctx], axis=1)
    mod = _ada(ct, w_ada[0], b_ada[0][None, :])

    w_l = w_in[0]
    w_in_b = jnp.concatenate([_head_layout(w_l[:, :A_W]), _head_layout(w_l[:, A_W:2 * A_W]),
                              w_l[:, 2 * A_W:]], axis=1).astype(bf)
    cos_t, sin_t = _rope_tables(N_TOK)
    g1 = norm1_g[0][None, :]

    kc, vct = _ctx_proj(ctx2d, mod, g1, w_in_b)
    hx, q, k, vt, sx = _in_proj(x2d, mod, g1, w_in_b, cos_t, sin_t,
                                w_spatial[0].astype(bf), b_spatial[0].T,
                                sgu_ln_g[0][None, :], sgu_ln_b[0][None, :])
    ax = _attn(q, k, vt, kc, vct, lam_q1, lam_k1, lam_q2, lam_k2, subln_g[0][:, None])
    m = _merge(hx, ax, sx, w_gate[0].astype(bf), b_gate[0][None, :],
               w_br_attn[0].astype(bf), w_br_sgu[0].astype(bf))
    x1, h2 = _out_proj(m, w_out[0].astype(bf), x2d, mod, norm2_g[0][None, :])
    out = _ffn(h2, w_ff1[0].astype(bf), w_ff2[0].astype(bf), x1, mod, final_g[None, :])
    return out[None]
```

```python
import functools
import math

import jax
import jax.numpy as jnp
from jax import lax
from jax.experimental import pallas as pl
from jax.experimental.pallas import tpu as pltpu

D = 2048
N_TOK = 8192
GRID_W = 64
CTX = 256
N_HEADS = 8
HEAD_DIM = 64
PAIR = 2 * HEAD_DIM
A_W = N_HEADS * PAIR
SGU_W = D // 2
N_GROUPS = 8
GROUP_DIM = SGU_W // N_GROUPS
CHUNK = 128
FF = 4 * D
IN_W = 3 * A_W + 2 * SGU_W
ROPE_BASE = 10000.0
EPS = 1e-6
LAM_INIT = 0.8 - 0.6 * math.exp(-0.3 * 0)
Q_SCALE = HEAD_DIM ** -0.5 * math.log2(math.e)

V7X_VMEM_BYTES = 64 * 1024 * 1024
VMEM_LIMIT = V7X_VMEM_BYTES - 8 * 1024 * 1024
LANES = 128

TM_IN = 512
TQ = 256
TK = 768
N_SLOTS = 3
DOT_ROWS = 256
SOFTMAX_ROWS = 64
GROWTH_LIMIT = 64.0
TM_MERGE = 1024
TN_MERGE = 512
TM_OUT = 512
TM_FF = 512
TF = 1024
TN_ADA = 1024

NEG_BIG = -1e30


def _params(*sem, flags=None):
    return pltpu.CompilerParams(dimension_semantics=sem, vmem_limit_bytes=VMEM_LIMIT, flags=flags)


def _gelu_tanh(x):
    return 0.5 * x * (1.0 + jnp.tanh(math.sqrt(2.0 / math.pi) * (x + 0.044715 * (x * x * x))))


def _mod_norm(xf, g, shift, scale):
    y = xf * lax.rsqrt(jnp.mean(xf * xf, axis=-1, keepdims=True) + EPS)
    return (y * g) * (1.0 + scale) + shift


def _ada_kernel(ct_ref, w_ref, b_ref, o_ref):
    w = w_ref[...]
    rows = []
    for r in range(2):
        cv = ct_ref[:, r:r + 1]
        a = cv * jax.nn.sigmoid(cv)
        rows.append(jnp.sum(a * w, axis=0, keepdims=True))
    o_ref[...] = jnp.concatenate(rows, axis=0) + b_ref[...]


def _ada(ct, w, b):
    n_out = w.shape[1]
    return pl.pallas_call(
        _ada_kernel,
        grid=(n_out // TN_ADA,),
        in_specs=[
            pl.BlockSpec((D, 2), lambda j: (0, 0)),
            pl.BlockSpec((D, TN_ADA), lambda j: (0, j)),
            pl.BlockSpec((1, TN_ADA), lambda j: (0, j)),
        ],
        out_specs=pl.BlockSpec((2, TN_ADA), lambda j: (0, j)),
        out_shape=jax.ShapeDtypeStruct((2, n_out), jnp.float32),
        compiler_params=_params("parallel"),
    )(ct, w, b)


def _ctx_kernel(x_ref, mod_ref, g_ref, w_ref, k_ref, vt_ref, h_s):
    j = pl.program_id(0)

    @pl.when(j == 0)
    def _():
        h = _mod_norm(x_ref[...], g_ref[...], mod_ref[1:2, 0:D], mod_ref[1:2, D:2 * D])
        h_s[...] = h.astype(jnp.bfloat16)

    acc = jnp.dot(h_s[...], w_ref[...], preferred_element_type=jnp.float32)

    @pl.when(j == 0)
    def _():
        k_ref[...] = acc.astype(jnp.bfloat16)

    @pl.when(j == 1)
    def _():
        vt_ref[...] = acc.T.astype(jnp.bfloat16)


def _ctx_proj(ctx2d, mod, g1, w_in):
    return pl.pallas_call(
        _ctx_kernel,
        grid=(2,),
        in_specs=[
            pl.BlockSpec((CTX, D), lambda j: (0, 0)),
            pl.BlockSpec((2, 6 * D), lambda j: (0, 0)),
            pl.BlockSpec((1, D), lambda j: (0, 0)),
            pl.BlockSpec((D, A_W), lambda j: (0, j + 1)),
        ],
        out_specs=[
            pl.BlockSpec((CTX, A_W), lambda j: (0, 0)),
            pl.BlockSpec((A_W, CTX), lambda j: (0, 0)),
        ],
        out_shape=[
            jax.ShapeDtypeStruct((CTX, A_W), jnp.bfloat16),
            jax.ShapeDtypeStruct((A_W, CTX), jnp.bfloat16),
        ],
        scratch_shapes=[pltpu.VMEM((CTX, D), jnp.bfloat16)],
        compiler_params=_params("arbitrary"),
    )(ctx2d, mod, g1, w_in)


def _rope(t, c, s):
    outs = []
    for h in range(N_HEADS):
        blk = t[:, h * PAIR:(h + 1) * PAIR]
        outs.append(blk * c + pltpu.roll(blk, PAIR // 2, axis=1) * s)
    return jnp.concatenate(outs, axis=1)


def _in_kernel(x_ref, mod_ref, g_ref, w_ref, cos_ref, sin_ref, ws_ref, bst_ref, lng_ref, lnb_ref,
               hx_ref, q_ref, k_ref, vt_ref, sx_ref, u_s):
    j = pl.program_id(1)

    @pl.when(j == 0)
    def _():
        h = _mod_norm(x_ref[...], g_ref[...], mod_ref[0:1, 0:D], mod_ref[0:1, D:2 * D])
        hx_ref[...] = h.astype(jnp.bfloat16)

    acc = jnp.dot(hx_ref[...], w_ref[...], preferred_element_type=jnp.float32)

    @pl.when(j == 0)
    def _():
        q_ref[...] = (_rope(acc, cos_ref[...], sin_ref[...]) * Q_SCALE).astype(jnp.bfloat16)

    @pl.when(j == 1)
    def _():
        k_ref[...] = _rope(acc, cos_ref[...], sin_ref[...]).astype(jnp.bfloat16)

    @pl.when(j == 2)
    def _():
        vt_ref[...] = acc.T.astype(jnp.bfloat16)

    @pl.when(j == 3)
    def _():
        u_s[...] = _gelu_tanh(acc)

    @pl.when(j == 4)
    def _():
        v = _gelu_tanh(acc)
        mu = jnp.mean(v, axis=-1, keepdims=True)
        vc = v - mu
        var = jnp.mean(vc * vc, axis=-1, keepdims=True)
        vn = ((vc * lax.rsqrt(var + EPS)) * lng_ref[...] + lnb_ref[...]).astype(jnp.bfloat16)
        n_chunks = TM_IN // CHUNK
        for g in range(N_GROUPS):
            cols = slice(g * GROUP_DIM, (g + 1) * GROUP_DIM)
            rhs = jnp.concatenate([vn[c * CHUNK:(c + 1) * CHUNK, cols] for c in range(n_chunks)], axis=1)
            mixed = jnp.dot(ws_ref[g], rhs, preferred_element_type=jnp.float32) + bst_ref[:, g:g + 1]
            for c in range(n_chunks):
                rows = slice(c * CHUNK, (c + 1) * CHUNK)
                sx_ref[rows, cols] = (u_s[rows, cols] * mixed[:, c * GROUP_DIM:(c + 1) * GROUP_DIM]
                                      ).astype(jnp.bfloat16)


def _in_proj(x2d, mod, g1, w_in, cos_t, sin_t, ws, bst, lng, lnb):
    n = x2d.shape[0]
    row = lambda i, j: (i, 0)
    const = lambda i, j: (0, 0)
    return pl.pallas_call(
        _in_kernel,
        grid=(n // TM_IN, IN_W // A_W),
        in_specs=[
            pl.BlockSpec((TM_IN, D), row),
            pl.BlockSpec((2, 6 * D), const),
            pl.BlockSpec((1, D), const),
            pl.BlockSpec((D, A_W), lambda i, j: (0, j)),
            pl.BlockSpec((TM_IN, PAIR), row),
            pl.BlockSpec((TM_IN, PAIR), row),
            pl.BlockSpec((N_GROUPS, CHUNK, CHUNK), lambda i, j: (0, 0, 0)),
            pl.BlockSpec((CHUNK, N_GROUPS), const),
            pl.BlockSpec((1, SGU_W), const),
            pl.BlockSpec((1, SGU_W), const),
        ],
        out_specs=[
            pl.BlockSpec((TM_IN, D), row),
            pl.BlockSpec((TM_IN, A_W), row),
            pl.BlockSpec((TM_IN, A_W), row),
            pl.BlockSpec((A_W, TM_IN), lambda i, j: (0, i)),
            pl.BlockSpec((TM_IN, SGU_W), row),
        ],
        out_shape=[
            jax.ShapeDtypeStruct((n, D), jnp.bfloat16),
            jax.ShapeDtypeStruct((n, A_W), jnp.bfloat16),
            jax.ShapeDtypeStruct((n, A_W), jnp.bfloat16),
            jax.ShapeDtypeStruct((A_W, n), jnp.bfloat16),
            jax.ShapeDtypeStruct((n, SGU_W), jnp.bfloat16),
        ],
        scratch_shapes=[pltpu.VMEM((TM_IN, SGU_W), jnp.float32)],
        compiler_params=_params("parallel", "arbitrary"),
    )(x2d, mod, g1, w_in, cos_t, sin_t, ws, bst, lng, lnb)


def _attn_kernel(q_ref, k_ref, vt_ref, kc_ref, vct_ref, lq1_ref, lk1_ref, lq2_ref, lk2_ref, sg_ref,
                 o_ref, kk_s, vv_s, p_s, acc_s, rise_s):
    n_lat = k_ref.shape[0]
    n_tiles = q_ref.shape[0] // TQ
    n_blocks = kk_s.shape[0] // TK
    last = n_blocks - 1
    width = 2 * TQ
    stat = (1, width)

    kk_s[0:n_lat, :] = k_ref[...]
    kk_s[n_lat:, :] = kc_ref[...]
    vv_s[:, 0:n_lat] = vt_ref[...]
    vv_s[:, n_lat:] = vct_ref[...]

    lam = (jnp.exp(jnp.sum(lq1_ref[...] * lk1_ref[...], axis=-1, keepdims=True))
           - jnp.exp(jnp.sum(lq2_ref[...] * lk2_ref[...], axis=-1, keepdims=True)) + LAM_INIT)

    def tile_rows(qi):
        return pl.ds(pl.multiple_of(qi * TQ, TQ), TQ)

    def masked_queries(qi):
        q = q_ref[tile_rows(qi), :]
        lane = lax.broadcasted_iota(jnp.int32, q.shape, 1)
        first = (lane % HEAD_DIM) < (HEAD_DIM // 2)
        zero = jnp.zeros_like(q)
        return jnp.concatenate([jnp.where(first, q, zero), jnp.where(first, zero, q)], axis=0)

    def raw_scores(qq, start, rows):
        return lax.dot_general(kk_s[pl.ds(start, rows), :], qq, (((1,), (1,)), ((), ())),
                               preferred_element_type=jnp.float32)

    def fold8(x, op):
        return op(x.reshape(x.shape[0] // 8, 8, width), axis=0)

    def values(t, p, alpha):
        pv = jnp.dot(vv_s[:, pl.ds(pl.multiple_of(t * TK, TK), TK)], p, preferred_element_type=jnp.float32)
        acc_s[...] = pv if alpha is None else alpha * acc_s[...] + pv

    def finish(qi, alpha, l, rise):
        values(last, p_s[last % N_SLOTS], alpha)
        rise_s[qi] = jnp.max(rise)
        normalise_and_store(qi, l)

    def normalise_and_store(qi, l):
        o_both = acc_s[...] * (1.0 / l)
        o = o_both[:, :TQ] - lam * o_both[:, TQ:]
        y = o * lax.rsqrt(jnp.mean(o * o, axis=0, keepdims=True) + EPS)
        y = (y * sg_ref[...]) * (1.0 - LAM_INIT)
        o_ref[tile_rows(qi), :] = y.T.astype(jnp.bfloat16)

    def probs(qq, t, slot, stab):
        bm8 = jnp.full((8, width), NEG_BIG, jnp.float32)
        ls8 = jnp.zeros((8, width), jnp.float32)
        for c in range(TK // DOT_ROWS):
            s = raw_scores(qq, t * TK + c * DOT_ROWS, DOT_ROWS)
            for r in range(DOT_ROWS // SOFTMAX_ROWS):
                sc = s[r * SOFTMAX_ROWS:(r + 1) * SOFTMAX_ROWS]
                p = jnp.exp2(sc - stab)
                row0 = c * DOT_ROWS + r * SOFTMAX_ROWS
                p_s[slot, row0:row0 + SOFTMAX_ROWS, :] = p.astype(jnp.bfloat16)
                bm8 = jnp.maximum(bm8, fold8(sc, jnp.max))
                ls8 = ls8 + fold8(p, jnp.sum)
        return jnp.max(bm8, axis=0, keepdims=True), jnp.sum(ls8, axis=0, keepdims=True)

    def tile(qi):
        qq = masked_queries(qi)
        stab = jnp.max(raw_scores(qq, 0, DOT_ROWS), axis=0, keepdims=True)
        bm, l = probs(qq, 0, 0, stab)
        alpha = None
        rise = jnp.zeros(stat, jnp.float32)
        for t in range(last):
            stab_next = jnp.maximum(stab, bm)
            alpha_next = jnp.exp2(stab - stab_next)
            rise = jnp.maximum(rise, bm - stab)
            bm, ls = probs(qq, t + 1, (t + 1) % N_SLOTS, stab_next)
            values(t, p_s[t % N_SLOTS], alpha)
            stab, alpha, l = stab_next, alpha_next, alpha_next * l + ls
        return alpha, l, jnp.maximum(rise, bm - stab)

    def body(qi, pending):
        finish(qi - 1, *pending)
        return tile(qi)

    finish(n_tiles - 1, *lax.fori_loop(1, n_tiles, body, tile(0)))

    def redo(qi, carry):
        @pl.when(rise_s[qi] > GROWTH_LIMIT)
        def _():
            qq = masked_queries(qi)

            def safe_block(t, ml):
                m, l = ml
                s = raw_scores(qq, pl.multiple_of(t * TK, TK), TK)
                m_new = jnp.maximum(m, jnp.max(s, axis=0, keepdims=True))
                alpha = jnp.exp2(m - m_new)
                p = jnp.exp2(s - m_new)
                values(t, p.astype(jnp.bfloat16), alpha)
                return m_new, alpha * l + jnp.sum(p, axis=0, keepdims=True)

            acc_s[...] = jnp.zeros(acc_s.shape, jnp.float32)
            _, l_safe = lax.fori_loop(0, n_blocks, safe_block,
                                      (jnp.full(stat, NEG_BIG, jnp.float32), jnp.zeros(stat, jnp.float32)))
            normalise_and_store(qi, l_safe)
        return carry

    lax.fori_loop(0, n_tiles, redo, 0)


def _attn(q, k, vt, kc, vct, lq1, lk1, lq2, lk2, sg_col):
    n = q.shape[0]
    n_keys = n + CTX
    assert n % TQ == 0 and n_keys % TK == 0 and n_keys // TK >= 2
    vec = pl.BlockSpec((1, HEAD_DIM), lambda h: (0, 0))
    return pl.pallas_call(
        _attn_kernel,
        grid=(N_HEADS,),
        in_specs=[
            pl.BlockSpec((n, PAIR), lambda h: (0, h)),
            pl.BlockSpec((n, PAIR), lambda h: (0, h)),
            pl.BlockSpec((PAIR, n), lambda h: (h, 0)),
            pl.BlockSpec((CTX, PAIR), lambda h: (0, h)),
            pl.BlockSpec((PAIR, CTX), lambda h: (h, 0)),
            vec, vec, vec, vec,
            pl.BlockSpec((PAIR, 1), lambda h: (0, 0)),
        ],
        out_specs=pl.BlockSpec((n, PAIR), lambda h: (0, h)),
        out_shape=jax.ShapeDtypeStruct((n, A_W), jnp.bfloat16),
        scratch_shapes=[
            pltpu.VMEM((n_keys, PAIR), jnp.bfloat16),
            pltpu.VMEM((PAIR, n_keys), jnp.bfloat16),
            pltpu.VMEM((N_SLOTS, TK, 2 * TQ), jnp.bfloat16),
            pltpu.VMEM((PAIR, 2 * TQ), jnp.float32),
            pltpu.SMEM((n // TQ,), jnp.float32),
        ],
        compiler_params=_params("parallel"),
    )(q, k, vt, kc, vct, lq1, lk1, lq2, lk2, sg_col)


def _merge_kernel(hx_ref, ax_ref, sx_ref, wga_ref, wgs_ref, bga_ref, bgs_ref, wba_ref, wbs_ref, m_ref):
    hx = hx_ref[...]
    ga = jax.nn.sigmoid(jnp.dot(hx, wga_ref[...], preferred_element_type=jnp.float32) + bga_ref[...])
    gs = jax.nn.sigmoid(jnp.dot(hx, wgs_ref[...], preferred_element_type=jnp.float32) + bgs_ref[...])
    ba = jnp.dot(ax_ref[...], wba_ref[...], preferred_element_type=jnp.float32)
    bs = jnp.dot(sx_ref[...], wbs_ref[...], preferred_element_type=jnp.float32)
    m_ref[...] = (ga * ba + gs * bs).astype(jnp.bfloat16)


def _merge(hx, ax, sx, w_gate, b_gate, w_ba, w_bs):
    n = hx.shape[0]
    nj = D // TN_MERGE
    row = lambda i, j: (i, 0)
    return pl.pallas_call(
        _merge_kernel,
        grid=(n // TM_MERGE, nj),
        in_specs=[
            pl.BlockSpec((TM_MERGE, D), row),
            pl.BlockSpec((TM_MERGE, A_W), row),
            pl.BlockSpec((TM_MERGE, SGU_W), row),
            pl.BlockSpec((D, TN_MERGE), lambda i, j: (0, j)),
            pl.BlockSpec((D, TN_MERGE), lambda i, j: (0, j + nj)),
            pl.BlockSpec((1, TN_MERGE), lambda i, j: (0, j)),
            pl.BlockSpec((1, TN_MERGE), lambda i, j: (0, j + nj)),
            pl.BlockSpec((A_W, TN_MERGE), lambda i, j: (0, j)),
            pl.BlockSpec((SGU_W, TN_MERGE), lambda i, j: (0, j)),
        ],
        out_specs=pl.BlockSpec((TM_MERGE, TN_MERGE), lambda i, j: (i, j)),
        out_shape=jax.ShapeDtypeStruct((n, D), jnp.bfloat16),
        compiler_params=_params("parallel", "arbitrary"),
    )(hx, ax, sx, w_gate, w_gate, b_gate, b_gate, w_ba, w_bs)


def _out_kernel(m_ref, w_ref, x_ref, mod_ref, g_ref, x1_ref, h2_ref):
    y = jnp.dot(m_ref[...], w_ref[...], preferred_element_type=jnp.float32)
    x1 = x_ref[...] + mod_ref[0:1, 2 * D:3 * D] * y
    x1_ref[...] = x1
    h2 = _mod_norm(x1, g_ref[...], mod_ref[0:1, 3 * D:4 * D], mod_ref[0:1, 4 * D:5 * D])
    h2_ref[...] = h2.astype(jnp.bfloat16)


def _out_proj(m, w_out, x2d, mod, g2):
    n = m.shape[0]
    row = lambda i: (i, 0)
    const = lambda i: (0, 0)
    return pl.pallas_call(
        _out_kernel,
        grid=(n // TM_OUT,),
        in_specs=[
            pl.BlockSpec((TM_OUT, D), row),
            pl.BlockSpec((D, D), const),
            pl.BlockSpec((TM_OUT, D), row),
            pl.BlockSpec((2, 6 * D), const),
            pl.BlockSpec((1, D), const),
        ],
        out_specs=[pl.BlockSpec((TM_OUT, D), row), pl.BlockSpec((TM_OUT, D), row)],
        out_shape=[jax.ShapeDtypeStruct((n, D), jnp.float32), jax.ShapeDtypeStruct((n, D), jnp.bfloat16)],
        compiler_params=_params("parallel"),
    )(m, w_out, x2d, mod, g2)


def _ffn_kernel(h_ref, w1_ref, w2_ref, x1_ref, mod_ref, fg_ref, o_ref, acc_s):
    f = pl.program_id(1)

    @pl.when(f == 0)
    def _():
        acc_s[...] = jnp.zeros(acc_s.shape, jnp.float32)

    a = jnp.maximum(jnp.dot(h_ref[...], w1_ref[...], preferred_element_type=jnp.float32), 0.0)
    acc_s[...] += jnp.dot((a * a).astype(jnp.bfloat16), w2_ref[...], preferred_element_type=jnp.float32)

    @pl.when(f == pl.num_programs(1) - 1)
    def _():
        x2 = x1_ref[...] + mod_ref[0:1, 5 * D:6 * D] * acc_s[...]
        y = x2 * lax.rsqrt(jnp.mean(x2 * x2, axis=-1, keepdims=True) + EPS)
        o_ref[...] = y * fg_ref[...]


def _ffn(h2, w1, w2, x1, mod, fg):
    n = h2.shape[0]
    row = lambda i, f: (i, 0)
    const = lambda i, f: (0, 0)
    return pl.pallas_call(
        _ffn_kernel,
        grid=(n // TM_FF, FF // TF),
        in_specs=[
            pl.BlockSpec((TM_FF, D), row),
            pl.BlockSpec((D, TF), lambda i, f: (0, f)),
            pl.BlockSpec((TF, D), lambda i, f: (f, 0)),
            pl.BlockSpec((TM_FF, D), row),
            pl.BlockSpec((2, 6 * D), const),
            pl.BlockSpec((1, D), const),
        ],
        out_specs=pl.BlockSpec((TM_FF, D), row),
        out_shape=jax.ShapeDtypeStruct((n, D), jnp.float32),
        scratch_shapes=[pltpu.VMEM((TM_FF, D), jnp.float32)],
        compiler_params=_params("parallel", "arbitrary"),
    )(h2, w1, w2, x1, mod, fg)


def _head_layout(w):
    w = w.reshape(D, N_HEADS, 2, HEAD_DIM // 2, 2)
    return w.transpose(0, 1, 4, 2, 3).reshape(D, A_W)


def _rope_tables(n_tokens):
    rows = n_tokens // GRID_W
    r, col = jnp.meshgrid(jnp.arange(rows, dtype=jnp.float32),
                          jnp.arange(GRID_W, dtype=jnp.float32), indexing="ij")
    n_freq = HEAD_DIM // 4
    inv = ROPE_BASE ** (-jnp.arange(n_freq, dtype=jnp.float32) / n_freq)
    ang = jnp.concatenate([r.reshape(-1, 1) * inv, col.reshape(-1, 1) * inv], axis=-1)
    cos, sin = jnp.cos(ang), jnp.sin(ang)
    return jnp.tile(cos, (1, 4)), jnp.concatenate([-sin, -sin, sin, sin], axis=-1)


def kernel(x, c, ctx, c_ctx, w_ada, b_ada, norm1_g, norm2_g, w_in, lam_q1, lam_k1, lam_q2, lam_k2, subln_g, sgu_ln_g, sgu_ln_b, w_spatial, b_spatial, w_gate, b_gate, w_br_attn, w_br_sgu, w_out, w_ff1, w_ff2, final_g):
    assert x.shape == (1, N_TOK, D) and ctx.shape == (1, CTX, D) and w_ada.shape[0] == 1
    bf = jnp.bfloat16
    x2d = x[0]
    ctx2d = ctx[0]

    ct = jnp.stack([c[0], c_ctx], axis=1)
    mod = _ada(ct, w_ada[0], b_ada[0][None, :])

    w_l = w_in[0]
    w_in_b = jnp.concatenate([_head_layout(w_l[:, :A_W]), _head_layout(w_l[:, A_W:2 * A_W]),
                              w_l[:, 2 * A_W:]], axis=1).astype(bf)
    cos_t, sin_t = _rope_tables(N_TOK)
    g1 = norm1_g[0][None, :]

    kc, vct = _ctx_proj(ctx2d, mod, g1, w_in_b)
    hx, q, k, vt, sx = _in_proj(x2d, mod, g1, w_in_b, cos_t, sin_t,
                                w_spatial[0].astype(bf), b_spatial[0].T,
                                sgu_ln_g[0][None, :], sgu_ln_b[0][None, :])
    ax = _attn(q, k, vt, kc, vct, lam_q1, lam_k1, lam_q2, lam_k2, subln_g[0][:, None])
    m = _merge(hx, ax, sx, w_gate[0].astype(bf), b_gate[0][None, :],
               w_br_attn[0].astype(bf), w_br_sgu[0].astype(bf))
    x1, h2 = _out_proj(m, w_out[0].astype(bf), x2d, mod, norm2_g[0][None, :])
    out = _ffn(h2, w_ff1[0].astype(bf), w_ff2[0].astype(bf), x1, mod, final_g[None, :])
    return out[None]
```

```python
import functools
import math

import jax
import jax.numpy as jnp
from jax import lax
from jax.experimental import pallas as pl
from jax.experimental.pallas import tpu as pltpu

D = 2048
N_TOK = 8192
GRID_W = 64
CTX = 256
N_HEADS = 8
HEAD_DIM = 64
PAIR = 2 * HEAD_DIM
A_W = N_HEADS * PAIR
SGU_W = D // 2
N_GROUPS = 8
GROUP_DIM = SGU_W // N_GROUPS
CHUNK = 128
FF = 4 * D
IN_W = 3 * A_W + 2 * SGU_W
ROPE_BASE = 10000.0
EPS = 1e-6
LAM_INIT = 0.8 - 0.6 * math.exp(-0.3 * 0)
Q_SCALE = HEAD_DIM ** -0.5 * math.log2(math.e)

V7X_VMEM_BYTES = 64 * 1024 * 1024
VMEM_LIMIT = V7X_VMEM_BYTES - 8 * 1024 * 1024
LANES = 128

TM_IN = 512
TQ = 256
TK = 768
N_SLOTS = 3
DOT_ROWS = 256
SOFTMAX_ROWS = 64
GROWTH_LIMIT = 64.0
TM_MERGE = 1024
TN_MERGE = 512
TM_OUT = 512
TM_FF = 512
TF = 1024
TN_ADA = 1024

NEG_BIG = -1e30


def _params(*sem, flags=None):
    return pltpu.CompilerParams(dimension_semantics=sem, vmem_limit_bytes=VMEM_LIMIT, flags=flags)


def _gelu_tanh(x):
    return 0.5 * x * (1.0 + jnp.tanh(math.sqrt(2.0 / math.pi) * (x + 0.044715 * (x * x * x))))


def _mod_norm(xf, g, shift, scale):
    y = xf * lax.rsqrt(jnp.mean(xf * xf, axis=-1, keepdims=True) + EPS)
    return (y * g) * (1.0 + scale) + shift


def _ada_kernel(ct_ref, w_ref, b_ref, o_ref):
    w = w_ref[...]
    rows = []
    for r in range(2):
        cv = ct_ref[:, r:r + 1]
        a = cv * jax.nn.sigmoid(cv)
        rows.append(jnp.sum(a * w, axis=0, keepdims=True))
    o_ref[...] = jnp.concatenate(rows, axis=0) + b_ref[...]


def _ada(ct, w, b):
    n_out = w.shape[1]
    return pl.pallas_call(
        _ada_kernel,
        grid=(n_out // TN_ADA,),
        in_specs=[
            pl.BlockSpec((D, 2), lambda j: (0, 0)),
            pl.BlockSpec((D, TN_ADA), lambda j: (0, j)),
            pl.BlockSpec((1, TN_ADA), lambda j: (0, j)),
        ],
        out_specs=pl.BlockSpec((2, TN_ADA), lambda j: (0, j)),
        out_shape=jax.ShapeDtypeStruct((2, n_out), jnp.float32),
        compiler_params=_params("parallel"),
    )(ct, w, b)


def _ctx_kernel(x_ref, mod_ref, g_ref, w_ref, k_ref, vt_ref, h_s):
    j = pl.program_id(0)

    @pl.when(j == 0)
    def _():
        h = _mod_norm(x_ref[...], g_ref[...], mod_ref[1:2, 0:D], mod_ref[1:2, D:2 * D])
        h_s[...] = h.astype(jnp.bfloat16)

    acc = jnp.dot(h_s[...], w_ref[...], preferred_element_type=jnp.float32)

    @pl.when(j == 0)
    def _():
        k_ref[...] = acc.astype(jnp.bfloat16)

    @pl.when(j == 1)
    def _():
        vt_ref[...] = acc.T.astype(jnp.bfloat16)


def _ctx_proj(ctx2d, mod, g1, w_in):
    return pl.pallas_call(
        _ctx_kernel,
        grid=(2,),
        in_specs=[
            pl.BlockSpec((CTX, D), lambda j: (0, 0)),
            pl.BlockSpec((2, 6 * D), lambda j: (0, 0)),
            pl.BlockSpec((1, D), lambda j: (0, 0)),
            pl.BlockSpec((D, A_W), lambda j: (0, j + 1)),
        ],
        out_specs=[
            pl.BlockSpec((CTX, A_W), lambda j: (0, 0)),
            pl.BlockSpec((A_W, CTX), lambda j: (0, 0)),
        ],
        out_shape=[
            jax.ShapeDtypeStruct((CTX, A_W), jnp.bfloat16),
            jax.ShapeDtypeStruct((A_W, CTX), jnp.bfloat16),
        ],
        scratch_shapes=[pltpu.VMEM((CTX, D), jnp.bfloat16)],
        compiler_params=_params("arbitrary"),
    )(ctx2d, mod, g1, w_in)


def _rope(t, c, s):
    even = lax.broadcasted_iota(jnp.int32, (t.shape[0], LANES), 1) % 2 == 0
    outs = []
    for h in range(A_W // LANES):
        blk = t[:, h * LANES:(h + 1) * LANES]
        partner = jnp.where(even, pltpu.roll(blk, LANES - 1, axis=1), pltpu.roll(blk, 1, axis=1))
        outs.append(blk * c + partner * s)
    return jnp.concatenate(outs, axis=1)


def _in_kernel(x_ref, mod_ref, g_ref, w_ref, cos_ref, sin_ref, ws_ref, bst_ref, lng_ref, lnb_ref,
               hx_ref, q_ref, k_ref, vt_ref, sx_ref, u_s):
    j = pl.program_id(1)

    @pl.when(j == 0)
    def _():
        h = _mod_norm(x_ref[...], g_ref[...], mod_ref[0:1, 0:D], mod_ref[0:1, D:2 * D])
        hx_ref[...] = h.astype(jnp.bfloat16)

    acc = jnp.dot(hx_ref[...], w_ref[...], preferred_element_type=jnp.float32)

    @pl.when(j == 0)
    def _():
        q_ref[...] = (_rope(acc, cos_ref[...], sin_ref[...]) * Q_SCALE).astype(jnp.bfloat16)

    @pl.when(j == 1)
    def _():
        k_ref[...] = _rope(acc, cos_ref[...], sin_ref[...]).astype(jnp.bfloat16)

    @pl.when(j == 2)
    def _():
        vt_ref[...] = acc.T.astype(jnp.bfloat16)

    @pl.when(j == 3)
    def _():
        u_s[...] = _gelu_tanh(acc)

    @pl.when(j == 4)
    def _():
        v = _gelu_tanh(acc)
        mu = jnp.mean(v, axis=-1, keepdims=True)
        vc = v - mu
        var = jnp.mean(vc * vc, axis=-1, keepdims=True)
        vn = ((vc * lax.rsqrt(var + EPS)) * lng_ref[...] + lnb_ref[...]).astype(jnp.bfloat16)
        n_chunks = TM_IN // CHUNK
        for g in range(N_GROUPS):
            cols = slice(g * GROUP_DIM, (g + 1) * GROUP_DIM)
            rhs = jnp.concatenate([vn[c * CHUNK:(c + 1) * CHUNK, cols] for c in range(n_chunks)], axis=1)
            mixed = jnp.dot(ws_ref[g], rhs, preferred_element_type=jnp.float32) + bst_ref[:, g:g + 1]
            for c in range(n_chunks):
                rows = slice(c * CHUNK, (c + 1) * CHUNK)
                sx_ref[rows, cols] = (u_s[rows, cols] * mixed[:, c * GROUP_DIM:(c + 1) * GROUP_DIM]
                                      ).astype(jnp.bfloat16)


def _in_proj(x2d, mod, g1, w_in, cos_t, sin_t, ws, bst, lng, lnb):
    n = x2d.shape[0]
    row = lambda i, j: (i, 0)
    const = lambda i, j: (0, 0)
    return pl.pallas_call(
        _in_kernel,
        grid=(n // TM_IN, IN_W // A_W),
        in_specs=[
            pl.BlockSpec((TM_IN, D), row),
            pl.BlockSpec((2, 6 * D), const),
            pl.BlockSpec((1, D), const),
            pl.BlockSpec((D, A_W), lambda i, j: (0, j)),
            pl.BlockSpec((TM_IN, PAIR), row),
            pl.BlockSpec((TM_IN, PAIR), row),
            pl.BlockSpec((N_GROUPS, CHUNK, CHUNK), lambda i, j: (0, 0, 0)),
            pl.BlockSpec((CHUNK, N_GROUPS), const),
            pl.BlockSpec((1, SGU_W), const),
            pl.BlockSpec((1, SGU_W), const),
        ],
        out_specs=[
            pl.BlockSpec((TM_IN, D), row),
            pl.BlockSpec((TM_IN, A_W), row),
            pl.BlockSpec((TM_IN, A_W), row),
            pl.BlockSpec((A_W, TM_IN), lambda i, j: (0, i)),
            pl.BlockSpec((TM_IN, SGU_W), row),
        ],
        out_shape=[
            jax.ShapeDtypeStruct((n, D), jnp.bfloat16),
            jax.ShapeDtypeStruct((n, A_W), jnp.bfloat16),
            jax.ShapeDtypeStruct((n, A_W), jnp.bfloat16),
            jax.ShapeDtypeStruct((A_W, n), jnp.bfloat16),
            jax.ShapeDtypeStruct((n, SGU_W), jnp.bfloat16),
        ],
        scratch_shapes=[pltpu.VMEM((TM_IN, SGU_W), jnp.float32)],
        compiler_params=_params("parallel", "arbitrary"),
    )(x2d, mod, g1, w_in, cos_t, sin_t, ws, bst, lng, lnb)


def _attn_kernel(q_ref, k_ref, vt_ref, kc_ref, vct_ref, lq1_ref, lk1_ref, lq2_ref, lk2_ref, sg_ref,
                 o_ref, kk_s, vv_s, p_s, acc_s, rise_s):
    n_lat = k_ref.shape[0]
    n_tiles = q_ref.shape[0] // TQ
    n_blocks = kk_s.shape[0] // TK
    last = n_blocks - 1
    width = 2 * TQ
    stat = (1, width)

    kk_s[0:n_lat, :] = k_ref[...]
    kk_s[n_lat:, :] = kc_ref[...]
    vv_s[:, 0:n_lat] = vt_ref[...]
    vv_s[:, n_lat:] = vct_ref[...]

    lam = (jnp.exp(jnp.sum(lq1_ref[...] * lk1_ref[...], axis=-1, keepdims=True))
           - jnp.exp(jnp.sum(lq2_ref[...] * lk2_ref[...], axis=-1, keepdims=True)) + LAM_INIT)

    def tile_rows(qi):
        return pl.ds(pl.multiple_of(qi * TQ, TQ), TQ)

    def masked_queries(qi):
        q = q_ref[tile_rows(qi), :]
        lane = lax.broadcasted_iota(jnp.int32, q.shape, 1)
        first = lane < HEAD_DIM
        zero = jnp.zeros_like(q)
        return jnp.concatenate([jnp.where(first, q, zero), jnp.where(first, zero, q)], axis=0)

    def raw_scores(qq, start, rows):
        return lax.dot_general(kk_s[pl.ds(start, rows), :], qq, (((1,), (1,)), ((), ())),
                               preferred_element_type=jnp.float32)

    def fold8(x, op):
        return op(x.reshape(x.shape[0] // 8, 8, width), axis=0)

    def values(t, p, alpha):
        pv = jnp.dot(vv_s[:, pl.ds(pl.multiple_of(t * TK, TK), TK)], p, preferred_element_type=jnp.float32)
        acc_s[...] = pv if alpha is None else alpha * acc_s[...] + pv

    def finish(qi, alpha, l, rise):
        values(last, p_s[last % N_SLOTS], alpha)
        rise_s[qi] = jnp.max(rise)
        normalise_and_store(qi, l)

    def normalise_and_store(qi, l):
        o_both = acc_s[...] * (1.0 / l)
        o = o_both[:, :TQ] - lam * o_both[:, TQ:]
        y = o * lax.rsqrt(jnp.mean(o * o, axis=0, keepdims=True) + EPS)
        y = (y * sg_ref[...]) * (1.0 - LAM_INIT)
        o_ref[tile_rows(qi), :] = y.T.astype(jnp.bfloat16)

    def probs(qq, t, slot, stab):
        bm8 = jnp.full((8, width), NEG_BIG, jnp.float32)
        ls8 = jnp.zeros((8, width), jnp.float32)
        for c in range(TK // DOT_ROWS):
            s = raw_scores(qq, t * TK + c * DOT_ROWS, DOT_ROWS)
            for r in range(DOT_ROWS // SOFTMAX_ROWS):
                sc = s[r * SOFTMAX_ROWS:(r + 1) * SOFTMAX_ROWS]
                p = jnp.exp2(sc - stab)
                row0 = c * DOT_ROWS + r * SOFTMAX_ROWS
                p_s[slot, row0:row0 + SOFTMAX_ROWS, :] = p.astype(jnp.bfloat16)
                bm8 = jnp.maximum(bm8, fold8(sc, jnp.max))
                ls8 = ls8 + fold8(p, jnp.sum)
        return jnp.max(bm8, axis=0, keepdims=True), jnp.sum(ls8, axis=0, keepdims=True)

    def tile(qi):
        qq = masked_queries(qi)
        stab = jnp.max(raw_scores(qq, 0, DOT_ROWS), axis=0, keepdims=True)
        bm, l = probs(qq, 0, 0, stab)
        alpha = None
        rise = jnp.zeros(stat, jnp.float32)
        for t in range(last):
            stab_next = jnp.maximum(stab, bm)
            alpha_next = jnp.exp2(stab - stab_next)
            rise = jnp.maximum(rise, bm - stab)
            bm, ls = probs(qq, t + 1, (t + 1) % N_SLOTS, stab_next)
            values(t, p_s[t % N_SLOTS], alpha)
            stab, alpha, l = stab_next, alpha_next, alpha_next * l + ls
        return alpha, l, jnp.maximum(rise, bm - stab)

    def body(qi, pending):
        finish(qi - 1, *pending)
        return tile(qi)

    finish(n_tiles - 1, *lax.fori_loop(1, n_tiles, body, tile(0)))

    def redo(qi, carry):
        @pl.when(rise_s[qi] > GROWTH_LIMIT)
        def _():
            qq = masked_queries(qi)

            def safe_block(t, ml):
                m, l = ml
                s = raw_scores(qq, pl.multiple_of(t * TK, TK), TK)
                m_new = jnp.maximum(m, jnp.max(s, axis=0, keepdims=True))
                alpha = jnp.exp2(m - m_new)
                p = jnp.exp2(s - m_new)
                values(t, p.astype(jnp.bfloat16), alpha)
                return m_new, alpha * l + jnp.sum(p, axis=0, keepdims=True)

            acc_s[...] = jnp.zeros(acc_s.shape, jnp.float32)
            _, l_safe = lax.fori_loop(0, n_blocks, safe_block,
                                      (jnp.full(stat, NEG_BIG, jnp.float32), jnp.zeros(stat, jnp.float32)))
            normalise_and_store(qi, l_safe)
        return carry

    lax.fori_loop(0, n_tiles, redo, 0)


def _attn(q, k, vt, kc, vct, lq1, lk1, lq2, lk2, sg_col):
    n = q.shape[0]
    n_keys = n + CTX
    assert n % TQ == 0 and n_keys % TK == 0 and n_keys // TK >= 2
    vec = pl.BlockSpec((1, HEAD_DIM), lambda h: (0, 0))
    return pl.pallas_call(
        _attn_kernel,
        grid=(N_HEADS,),
        in_specs=[
            pl.BlockSpec((n, PAIR), lambda h: (0, h)),
            pl.BlockSpec((n, PAIR), lambda h: (0, h)),
            pl.BlockSpec((PAIR, n), lambda h: (h, 0)),
            pl.BlockSpec((CTX, PAIR), lambda h: (0, h)),
            pl.BlockSpec((PAIR, CTX), lambda h: (h, 0)),
            vec, vec, vec, vec,
            pl.BlockSpec((PAIR, 1), lambda h: (0, 0)),
        ],
        out_specs=pl.BlockSpec((n, PAIR), lambda h: (0, h)),
        out_shape=jax.ShapeDtypeStruct((n, A_W), jnp.bfloat16),
        scratch_shapes=[
            pltpu.VMEM((n_keys, PAIR), jnp.bfloat16),
            pltpu.VMEM((PAIR, n_keys), jnp.bfloat16),
            pltpu.VMEM((N_SLOTS, TK, 2 * TQ), jnp.bfloat16),
            pltpu.VMEM((PAIR, 2 * TQ), jnp.float32),
            pltpu.SMEM((n // TQ,), jnp.float32),
        ],
        compiler_params=_params("parallel"),
    )(q, k, vt, kc, vct, lq1, lk1, lq2, lk2, sg_col)


def _merge_kernel(hx_ref, ax_ref, sx_ref, wga_ref, wgs_ref, bga_ref, bgs_ref, wba_ref, wbs_ref, m_ref):
    hx = hx_ref[...]
    ga = jax.nn.sigmoid(jnp.dot(hx, wga_ref[...], preferred_element_type=jnp.float32) + bga_ref[...])
    gs = jax.nn.sigmoid(jnp.dot(hx, wgs_ref[...], preferred_element_type=jnp.float32) + bgs_ref[...])
    ba = jnp.dot(ax_ref[...], wba_ref[...], preferred_element_type=jnp.float32)
    bs = jnp.dot(sx_ref[...], wbs_ref[...], preferred_element_type=jnp.float32)
    m_ref[...] = (ga * ba + gs * bs).astype(jnp.bfloat16)


def _merge(hx, ax, sx, w_gate, b_gate, w_ba, w_bs):
    n = hx.shape[0]
    nj = D // TN_MERGE
    row = lambda i, j: (i, 0)
    return pl.pallas_call(
        _merge_kernel,
        grid=(n // TM_MERGE, nj),
        in_specs=[
            pl.BlockSpec((TM_MERGE, D), row),
            pl.BlockSpec((TM_MERGE, A_W), row),
            pl.BlockSpec((TM_MERGE, SGU_W), row),
            pl.BlockSpec((D, TN_MERGE), lambda i, j: (0, j)),
            pl.BlockSpec((D, TN_MERGE), lambda i, j: (0, j + nj)),
            pl.BlockSpec((1, TN_MERGE), lambda i, j: (0, j)),
            pl.BlockSpec((1, TN_MERGE), lambda i, j: (0, j + nj)),
            pl.BlockSpec((A_W, TN_MERGE), lambda i, j: (0, j)),
            pl.BlockSpec((SGU_W, TN_MERGE), lambda i, j: (0, j)),
        ],
        out_specs=pl.BlockSpec((TM_MERGE, TN_MERGE), lambda i, j: (i, j)),
        out_shape=jax.ShapeDtypeStruct((n, D), jnp.bfloat16),
        compiler_params=_params("parallel", "arbitrary"),
    )(hx, ax, sx, w_gate, w_gate, b_gate, b_gate, w_ba, w_bs)


def _out_kernel(m_ref, w_ref, x_ref, mod_ref, g_ref, x1_ref, h2_ref):
    y = jnp.dot(m_ref[...], w_ref[...], preferred_element_type=jnp.float32)
    x1 = x_ref[...] + mod_ref[0:1, 2 * D:3 * D] * y
    x1_ref[...] = x1
    h2 = _mod_norm(x1, g_ref[...], mod_ref[0:1, 3 * D:4 * D], mod_ref[0:1, 4 * D:5 * D])
    h2_ref[...] = h2.astype(jnp.bfloat16)


def _out_proj(m, w_out, x2d, mod, g2):
    n = m.shape[0]
    row = lambda i: (i, 0)
    const = lambda i: (0, 0)
    return pl.pallas_call(
        _out_kernel,
        grid=(n // TM_OUT,),
        in_specs=[
            pl.BlockSpec((TM_OUT, D), row),
            pl.BlockSpec((D, D), const),
            pl.BlockSpec((TM_OUT, D), row),
            pl.BlockSpec((2, 6 * D), const),
            pl.BlockSpec((1, D), const),
        ],
        out_specs=[pl.BlockSpec((TM_OUT, D), row), pl.BlockSpec((TM_OUT, D), row)],
        out_shape=[jax.ShapeDtypeStruct((n, D), jnp.float32), jax.ShapeDtypeStruct((n, D), jnp.bfloat16)],
        compiler_params=_params("parallel"),
    )(m, w_out, x2d, mod, g2)


def _ffn_kernel(h_ref, w1_ref, w2_ref, x1_ref, mod_ref, fg_ref, o_ref, acc_s):
    f = pl.program_id(1)

    @pl.when(f == 0)
    def _():
        acc_s[...] = jnp.zeros(acc_s.shape, jnp.float32)

    a = jnp.maximum(jnp.dot(h_ref[...], w1_ref[...], preferred_element_type=jnp.float32), 0.0)
    acc_s[...] += jnp.dot((a * a).astype(jnp.bfloat16), w2_ref[...], preferred_element_type=jnp.float32)

    @pl.when(f == pl.num_programs(1) - 1)
    def _():
        x2 = x1_ref[...] + mod_ref[0:1, 5 * D:6 * D] * acc_s[...]
        y = x2 * lax.rsqrt(jnp.mean(x2 * x2, axis=-1, keepdims=True) + EPS)
        o_ref[...] = y * fg_ref[...]


def _ffn(h2, w1, w2, x1, mod, fg):
    n = h2.shape[0]
    row = lambda i, f: (i, 0)
    const = lambda i, f: (0, 0)
    return pl.pallas_call(
        _ffn_kernel,
        grid=(n // TM_FF, FF // TF),
        in_specs=[
            pl.BlockSpec((TM_FF, D), row),
            pl.BlockSpec((D, TF), lambda i, f: (0, f)),
            pl.BlockSpec((TF, D), lambda i, f: (f, 0)),
            pl.BlockSpec((TM_FF, D), row),
            pl.BlockSpec((2, 6 * D), const),
            pl.BlockSpec((1, D), const),
        ],
        out_specs=pl.BlockSpec((TM_FF, D), row),
        out_shape=jax.ShapeDtypeStruct((n, D), jnp.float32),
        scratch_shapes=[pltpu.VMEM((TM_FF, D), jnp.float32)],
        compiler_params=_params("parallel", "arbitrary"),
    )(h2, w1, w2, x1, mod, fg)


def _rope_tables(n_tokens):
    rows = n_tokens // GRID_W
    n_freq = HEAD_DIM // 4
    inv = ROPE_BASE ** (-jnp.arange(n_freq, dtype=jnp.float32) / n_freq)
    ang_r = jnp.arange(rows, dtype=jnp.float32)[:, None] * inv
    ang_c = jnp.arange(GRID_W, dtype=jnp.float32)[:, None] * inv

    def per_token(fn):
        tab = jnp.concatenate([jnp.broadcast_to(fn(ang_r)[:, None, :], (rows, GRID_W, n_freq)),
                               jnp.broadcast_to(fn(ang_c)[None, :, :], (rows, GRID_W, n_freq))], axis=-1)
        return jnp.tile(jnp.repeat(tab.reshape(n_tokens, 2 * n_freq), 2, axis=1), (1, LANES // HEAD_DIM))

    sign = jnp.where(jnp.arange(LANES) % 2 == 0, -1.0, 1.0).astype(jnp.float32)
    return per_token(jnp.cos), per_token(jnp.sin) * sign


def kernel(x, c, ctx, c_ctx, w_ada, b_ada, norm1_g, norm2_g, w_in, lam_q1, lam_k1, lam_q2, lam_k2, subln_g, sgu_ln_g, sgu_ln_b, w_spatial, b_spatial, w_gate, b_gate, w_br_attn, w_br_sgu, w_out, w_ff1, w_ff2, final_g):
    assert x.shape == (1, N_TOK, D) and ctx.shape == (1, CTX, D) and w_ada.shape[0] == 1
    bf = jnp.bfloat16
    x2d = x[0]
    ctx2d = ctx[0]

    ct = jnp.stack([c[0], c_ctx], axis=1)
    mod = _ada(ct, w_ada[0], b_ada[0][None, :])

    w_in_b = w_in[0].astype(bf)
    cos_t, sin_t = _rope_tables(N_TOK)
    g1 = norm1_g[0][None, :]

    kc, vct = _ctx_proj(ctx2d, mod, g1, w_in_b)
    hx, q, k, vt, sx = _in_proj(x2d, mod, g1, w_in_b, cos_t, sin_t,
                                w_spatial[0].astype(bf), b_spatial[0].T,
                                sgu_ln_g[0][None, :], sgu_ln_b[0][None, :])
    ax = _attn(q, k, vt, kc, vct, lam_q1, lam_k1, lam_q2, lam_k2, subln_g[0][:, None])
    m = _merge(hx, ax, sx, w_gate[0].astype(bf), b_gate[0][None, :],
               w_br_attn[0].astype(bf), w_br_sgu[0].astype(bf))
    x1, h2 = _out_proj(m, w_out[0].astype(bf), x2d, mod, norm2_g[0][None, :])
    out = _ffn(h2, w_ff1[0].astype(bf), w_ff2[0].astype(bf), x1, mod, final_g[None, :])
    return out[None]
```
